```python
import jax, jax.numpy as jnp
from jax import lax
import numpy as np

D_MODEL = 1024
BATCH = 8
SEQ = 2048
DEPTH = 1
DEC_BATCH = 128
DEC_SEQ = 1
PAST_LEN = 16384
PAGE_SIZE = 128

N_META = 16
MLA_HEADS = 8
Q_LORA = 384
KV_LORA = 256
NOPE_DIM = 64
ROPE_DIM = 32
V_DIM = 64
ROPE_THETA = 10000.0
Q_BLOCK = 128
RET_HEADS = 4
RET_DK = 128
RET_DV = 128
RET_CHUNK = 128
D_FF = 2816
CONV_W = 3
LN_EPS = 1e-5
RMS_EPS = 1e-6
ALPHA = (2 * DEPTH) ** 0.25
BETA = (8 * DEPTH) ** -0.25
IN_SIZES = (Q_LORA, KV_LORA, ROPE_DIM, RET_HEADS * RET_DK, RET_HEADS * RET_DK, RET_HEADS * RET_DV, RET_HEADS * RET_DV, D_MODEL, D_MODEL)
N_IN = Q_LORA + KV_LORA + ROPE_DIM + 2 * RET_HEADS * RET_DK + 2 * RET_HEADS * RET_DV + 2 * D_MODEL

kernel_name = 'mla_retention_convffn_hybrid_step'


def layer_norm(x, g, b):
    xf = x.astype(jnp.float32)
    mu = jnp.mean(xf, -1, keepdims=True)
    var = jnp.mean(jnp.square(xf - mu), -1, keepdims=True)
    return ((xf - mu) * lax.rsqrt(var + LN_EPS) * g.astype(jnp.float32) + b.astype(jnp.float32)).astype(x.dtype)


def rms_norm(x, g):
    xf = x.astype(jnp.float32)
    return (xf * lax.rsqrt(jnp.mean(jnp.square(xf), -1, keepdims=True) + RMS_EPS) * g.astype(jnp.float32)).astype(x.dtype)


def rope(x, pos):
    half = x.shape[-1] // 2
    inv = ROPE_THETA ** (-jnp.arange(half, dtype=jnp.float32) / half)
    ang = pos[:, None] * inv[None, :]
    cos = jnp.cos(ang)[:, None, :]
    sin = jnp.sin(ang)[:, None, :]
    xf = x.astype(jnp.float32)
    x1, x2 = xf[..., :half], xf[..., half:]
    return jnp.concatenate([x1 * cos - x2 * sin, x2 * cos + x1 * sin], -1).astype(x.dtype)


def mixer_inputs(xn, pos, w_in, q_norm_g, kv_norm_g, w_uq):
    b, t, _ = xn.shape
    h = xn @ w_in
    q_lat, c_raw, kr_raw, rq, rk, rv, rg, ga, gb = jnp.split(h, np.cumsum(IN_SIZES)[:-1].tolist(), axis=-1)
    q = jnp.einsum('btl,lhd->bthd', rms_norm(q_lat, q_norm_g), w_uq)
    q_nope = q[..., :NOPE_DIM]
    q_rope = rope(q[..., NOPE_DIM:], pos)
    c = rms_norm(c_raw, kv_norm_g)
    k_rope = rope(kr_raw[:, :, None, :], pos)[:, :, 0]
    rq = rope(rq.reshape(b, t, RET_HEADS, RET_DK), pos)
    rk = rope(rk.reshape(b, t, RET_HEADS, RET_DK), pos) * (RET_DK ** -0.5)
    rv = rv.reshape(b, t, RET_HEADS, RET_DV)
    return q_nope, q_rope, c, k_rope, rq, rk, rv, rg, ga, gb


def mla_prompt(q_nope, q_rope, c, k_rope, w_uk, w_uv):
    b, l = c.shape[:2]
    k_nope = jnp.einsum('blc,chn->blhn', c, w_uk)
    v = jnp.einsum('blc,chv->blhv', c, w_uv)
    n_blk = -(-l // Q_BLOCK)
    lq = n_blk * Q_BLOCK

    def blocks(q):
        q = jnp.pad(q, ((0, 0), (0, lq - l), (0, 0), (0, 0)))
        return jnp.swapaxes(q.reshape(b, n_blk, Q_BLOCK, q.shape[2], q.shape[3]), 0, 1)

    q_pos = jnp.arange(lq).reshape(n_blk, Q_BLOCK)
    k_pos = jnp.arange(l)
    scale = (NOPE_DIM + ROPE_DIM) ** -0.5

    def attend(args):
        qn, qr, qp = args
        s = jnp.einsum('bqhn,bkhn->bhqk', qn, k_nope) + jnp.einsum('bqhr,bkr->bhqk', qr, k_rope)
        s = jnp.where(k_pos[None, :] <= qp[:, None], s.astype(jnp.float32) * scale, -jnp.inf)
        p = jax.nn.softmax(s, axis=-1).astype(v.dtype)
        return jnp.einsum('bhqk,bkhv->bqhv', p, v)

    o = lax.map(attend, (blocks(q_nope), blocks(q_rope), q_pos))
    return jnp.swapaxes(o, 0, 1).reshape(b, lq, MLA_HEADS * V_DIM)[:, :l]


def mla_sample(q_nope, q_rope, c_new, kr_new, c_past, kr_past, w_uk, w_uv):
    b, t = c_new.shape[:2]
    past = c_past.shape[1]
    scale = (NOPE_DIM + ROPE_DIM) ** -0.5
    q_abs = jnp.einsum('bthn,chn->bthc', q_nope, w_uk)
    s_past = jnp.einsum('bthc,bsc->bhts', q_abs, c_past) + jnp.einsum('bthr,bsr->bhts', q_rope, kr_past)
    s_new = jnp.einsum('bthc,buc->bhtu', q_abs, c_new) + jnp.einsum('bthr,bur->bhtu', q_rope, kr_new)
    causal = jnp.arange(t)[None, :] <= jnp.arange(t)[:, None]
    s_new = jnp.where(causal, s_new.astype(jnp.float32), -jnp.inf)
    s = jnp.concatenate([s_past.astype(jnp.float32), s_new], -1) * scale
    p = jax.nn.softmax(s, axis=-1).astype(c_new.dtype)
    o_lat = jnp.einsum('bhts,bsc->bthc', p[..., :past], c_past) + jnp.einsum('bhtu,buc->bthc', p[..., past:], c_new)
    return jnp.einsum('bthc,chv->bthv', o_lat, w_uv).reshape(b, t, MLA_HEADS * V_DIM)


def ret_log_decay():
    return jnp.log(1.0 - 2.0 ** (-5.0 - jnp.arange(RET_HEADS, dtype=jnp.float32)))


def retention_chunk(state, q, k, v, log_g):
    c = q.shape[2]
    idx = jnp.arange(c, dtype=jnp.float32)
    lg = log_g[:, None, None]
    diff = idx[:, None] - idx[None, :]
    decay = jnp.where(diff >= 0, jnp.exp(jnp.maximum(diff, 0.0) * lg), 0.0)
    inner = jnp.einsum('bhik,bhjk->bhij', q, k) * decay
    o = jnp.einsum('bhij,bhjv->bhiv', inner, v) + jnp.einsum('bhik,bhkv->bhiv', q * jnp.exp((idx[:, None] + 1.0) * lg), state)
    k_dec = k * jnp.exp((c - 1.0 - idx[:, None]) * lg)
    new_state = jnp.exp(c * log_g)[:, None, None] * state + jnp.einsum('bhjk,bhjv->bhkv', k_dec, v)
    return new_state, o


def retention_prompt(rq, rk, rv, log_g):
    b, l = rq.shape[:2]
    pad = (-l) % RET_CHUNK
    n = (l + pad) // RET_CHUNK

    def chunks(x):
        x = jnp.pad(jnp.swapaxes(x.astype(jnp.float32), 1, 2), ((0, 0), (0, 0), (pad, 0), (0, 0)))
        return jnp.moveaxis(x.reshape(b, RET_HEADS, n, RET_CHUNK, x.shape[-1]), 2, 0)

    state0 = jnp.zeros((b, RET_HEADS, RET_DK, RET_DV), jnp.float32)
    state, o = lax.scan(lambda s, xs: retention_chunk(s, xs[0], xs[1], xs[2], log_g), state0, (chunks(rq), chunks(rk), chunks(rv)))
    o = jnp.moveaxis(o, 0, 2).reshape(b, RET_HEADS, n * RET_CHUNK, RET_DV)[:, :, pad:]
    return jnp.swapaxes(o, 1, 2), state


def retention_sample(rq, rk, rv, state, log_g):
    tr = lambda x: jnp.swapaxes(x.astype(jnp.float32), 1, 2)
    new_state, o = retention_chunk(state.astype(jnp.float32), tr(rq), tr(rk), tr(rv), log_g)
    return jnp.swapaxes(o, 1, 2), new_state


def retention_readout(o, rg, gn_g):
    b, t = o.shape[:2]
    mu = jnp.mean(o, -1, keepdims=True)
    var = jnp.mean(jnp.square(o - mu), -1, keepdims=True)
    on = ((o - mu) * lax.rsqrt(var + LN_EPS)).reshape(b, t, RET_HEADS * RET_DV) * gn_g.astype(jnp.float32)
    return (on * jax.nn.silu(rg.astype(jnp.float32))).astype(rg.dtype)


def mixer_merge(attn_o, ret_o, ga, gb, w_attn_out, w_ret_out, w_o):
    m = jax.nn.sigmoid(ga) * (attn_o @ w_attn_out) + jax.nn.sigmoid(gb) * (ret_o @ w_ret_out)
    return m @ w_o


def conv_ffn(x, prev, w_ffn_in, conv_w, conv_b, w_ffn_out):
    t = x.shape[1]
    a, g = jnp.split(x @ w_ffn_in, 2, axis=-1)
    ap = jnp.concatenate([prev.astype(a.dtype), a], axis=1)
    acc = conv_b
    for j in range(CONV_W):
        acc = acc + conv_w[j] * ap[:, j:j + t]
    h = jax.nn.gelu(acc) * g
    return h @ w_ffn_out, ap[:, -(CONV_W - 1):]


def setup_inputs(seed: int = 0) -> dict:
    key = jax.random.key(seed)
    ks = jax.random.split(key, 32)
    n_pages = PAST_LEN // PAGE_SIZE
    n_used = DEC_BATCH * n_pages
    n_phys = n_used + max(1, n_used // 4)
    nrm = lambda k, shape, s: jax.random.normal(k, shape, jnp.float32) * s
    gain = lambda k, shape: 1.0 + nrm(k, shape, 0.02)
    page_table = jax.random.permutation(ks[0], n_phys)[:n_used].reshape(DEC_BATCH, n_pages).astype(jnp.int32)
    hv = MLA_HEADS * V_DIM
    rdv = RET_HEADS * RET_DV
    return {
        'x_prompt': nrm(ks[1], (BATCH, SEQ, D_MODEL), 1.0),
        'x_sample': nrm(ks[2], (DEC_BATCH, DEC_SEQ, D_MODEL), 1.0),
        'cache_kv_latent': nrm(ks[3], (DEPTH, n_phys, PAGE_SIZE, KV_LORA), 1.0),
        'cache_k_rope': nrm(ks[4], (DEPTH, n_phys, PAGE_SIZE, ROPE_DIM), 1.0),
        'state_retention': nrm(ks[5], (DEPTH, DEC_BATCH, RET_HEADS, RET_DK, RET_DV), 0.5),
        'state_ffn_conv': nrm(ks[6], (DEPTH, DEC_BATCH, CONV_W - 1, D_FF), 1.0),
        'page_table': page_table,
        'meta_tokens': nrm(ks[7], (N_META, D_MODEL), 1.0),
        'ln0_g': gain(ks[8], (D_MODEL,)),
        'ln0_b': nrm(ks[9], (D_MODEL,), 0.02),
        'w_in': nrm(ks[10], (DEPTH, D_MODEL, N_IN), D_MODEL ** -0.5),
        'q_norm_g': gain(ks[11], (DEPTH, Q_LORA)),
        'kv_norm_g': gain(ks[12], (DEPTH, KV_LORA)),
        'w_uq': nrm(ks[13], (DEPTH, Q_LORA, MLA_HEADS, NOPE_DIM + ROPE_DIM), Q_LORA ** -0.5),
        'w_uk': nrm(ks[14], (DEPTH, KV_LORA, MLA_HEADS, NOPE_DIM), KV_LORA ** -0.5),
        'w_uv': nrm(ks[15], (DEPTH, KV_LORA, MLA_HEADS, V_DIM), KV_LORA ** -0.5),
        'ret_gn_g': gain(ks[16], (DEPTH, rdv)),
        'w_attn_out': nrm(ks[17], (DEPTH, hv, D_MODEL), hv ** -0.5),
        'w_ret_out': nrm(ks[18], (DEPTH, rdv, D_MODEL), rdv ** -0.5),
        'w_o': nrm(ks[19], (DEPTH, D_MODEL, D_MODEL), BETA * D_MODEL ** -0.5),
        'ln1_g': gain(ks[20], (DEPTH, D_MODEL)),
        'ln1_b': nrm(ks[21], (DEPTH, D_MODEL), 0.02),
        'w_ffn_in': nrm(ks[22], (DEPTH, D_MODEL, 2 * D_FF), D_MODEL ** -0.5),
        'conv_w': nrm(ks[23], (DEPTH, CONV_W, D_FF), CONV_W ** -0.5),
        'conv_b': nrm(ks[24], (DEPTH, D_FF), 0.02),
        'w_ffn_out': nrm(ks[25], (DEPTH, D_FF, D_MODEL), BETA * D_FF ** -0.5),
        'ln2_g': gain(ks[26], (DEPTH, D_MODEL)),
        'ln2_b': nrm(ks[27], (DEPTH, D_MODEL), 0.02),
    }


def reference(x_prompt, x_sample, cache_kv_latent, cache_k_rope, state_retention, state_ffn_conv, page_table, meta_tokens, ln0_g, ln0_b, w_in, q_norm_g, kv_norm_g, w_uq, w_uk, w_uv, ret_gn_g, w_attn_out, w_ret_out, w_o, ln1_g, ln1_b, w_ffn_in, conv_w, conv_b, w_ffn_out, ln2_g, ln2_b):
    b = x_prompt.shape[0]
    db, t = x_sample.shape[:2]
    past = page_table.shape[1] * PAGE_SIZE
    meta = jnp.broadcast_to(meta_tokens.astype(x_prompt.dtype)[None], (b, N_META, D_MODEL))
    xp = layer_norm(jnp.concatenate([meta, x_prompt], axis=1), ln0_g, ln0_b)
    xs = layer_norm(x_sample, ln0_g, ln0_b)
    pos_p = jnp.arange(xp.shape[1], dtype=jnp.float32)
    pos_s = past + jnp.arange(t, dtype=jnp.float32)
    log_g = ret_log_decay()
    conv0 = jnp.zeros((b, CONV_W - 1, D_FF), xp.dtype)
    kv_p, kr_p, rs_p, cs_p = [], [], [], []
    kv_s, kr_s, rs_s, cs_s = [], [], [], []
    for i in range(DEPTH):
        qn, qr, c, kr, rq, rk, rv, rg, ga, gb = mixer_inputs(xp, pos_p, w_in[i], q_norm_g[i], kv_norm_g[i], w_uq[i])
        a_o = mla_prompt(qn, qr, c, kr, w_uk[i], w_uv[i])
        r_o, r_state = retention_prompt(rq, rk, rv, log_g)
        mix = mixer_merge(a_o, retention_readout(r_o, rg, ret_gn_g[i]), ga, gb, w_attn_out[i], w_ret_out[i], w_o[i])
        xp = layer_norm(ALPHA * xp + mix, ln1_g[i], ln1_b[i])
        f, c_state = conv_ffn(xp, conv0, w_ffn_in[i], conv_w[i], conv_b[i], w_ffn_out[i])
        xp = layer_norm(ALPHA * xp + f, ln2_g[i], ln2_b[i])
        kv_p.append(c)
        kr_p.append(kr)
        rs_p.append(r_state.astype(state_retention.dtype))
        cs_p.append(c_state)
        qn, qr, c, kr, rq, rk, rv, rg, ga, gb = mixer_inputs(xs, pos_s, w_in[i], q_norm_g[i], kv_norm_g[i], w_uq[i])
        c_past = cache_kv_latent[i][page_table].reshape(db, past, KV_LORA).astype(c.dtype)
        kr_past = cache_k_rope[i][page_table].reshape(db, past, ROPE_DIM).astype(kr.dtype)
        a_o = mla_sample(qn, qr, c, kr, c_past, kr_past, w_uk[i], w_uv[i])
        r_o, r_state = retention_sample(rq, rk, rv, state_retention[i], log_g)
        mix = mixer_merge(a_o, retention_readout(r_o, rg, ret_gn_g[i]), ga, gb, w_attn_out[i], w_ret_out[i], w_o[i])
        xs = layer_norm(ALPHA * xs + mix, ln1_g[i], ln1_b[i])
        f, c_state = conv_ffn(xs, state_ffn_conv[i], w_ffn_in[i], conv_w[i], conv_b[i], w_ffn_out[i])
        xs = layer_norm(ALPHA * xs + f, ln2_g[i], ln2_b[i])
        kv_s.append(c)
        kr_s.append(kr)
        rs_s.append(r_state.astype(state_retention.dtype))
        cs_s.append(c_state.astype(state_ffn_conv.dtype))
    y_prompt = xp[:, N_META:]
    y_sample = xs
    return (y_prompt, y_sample, jnp.stack(kv_p), jnp.stack(kr_p), jnp.stack(rs_p), jnp.stack(cs_p), jnp.stack(kv_s), jnp.stack(kr_s), jnp.stack(rs_s), jnp.stack(cs_s))
```

```python
import functools

import numpy as np
import jax
import jax.numpy as jnp
from jax import lax
from jax.experimental import pallas as pl
from jax.experimental.pallas import tpu as pltpu

F32 = jnp.float32
BF16 = jnp.bfloat16

D_MODEL = 1024
N_META = 16
MLA_HEADS = 8
Q_LORA = 384
KV_LORA = 256
NOPE_DIM = 64
ROPE_DIM = 32
V_DIM = 64
ROPE_THETA = 10000.0
RET_HEADS = 4
RET_DK = 128
RET_DV = 128
D_FF = 2816
CONV_W = 3
PAGE_SIZE = 128
LN_EPS = 1e-5
RMS_EPS = 1e-6
DEPTH = 1
ALPHA = (2 * DEPTH) ** 0.25

LANES = 128
TILE = 128
FRONT_PAD = TILE - N_META
RET_W = RET_HEADS * RET_DK
HEAD_W = MLA_HEADS * LANES
N_TAB = 7
W1_COLS = Q_LORA + KV_LORA + 3 * RET_W + 2 * LANES
W2_COLS = RET_W + 2 * D_MODEL
NEG = -1e30
PAGES_PER_CHUNK = 16
VMEM_LIMIT = 56 * 1024 * 1024


def _dot(a, b):
    return jnp.dot(a, b, preferred_element_type=F32)


def _dot_nt(a, b):
    return lax.dot_general(a, b, (((1,), (1,)), ((), ())), preferred_element_type=F32)


def _layer_norm(x, g, b):
    mu = jnp.mean(x, -1, keepdims=True)
    xc = x - mu
    var = jnp.mean(xc * xc, -1, keepdims=True)
    return xc * lax.rsqrt(var + LN_EPS) * g + b


def _rms_norm(x, g):
    return x * lax.rsqrt(jnp.mean(x * x, -1, keepdims=True) + RMS_EPS) * g


def _group_norm(o):
    mu = jnp.mean(o, -1, keepdims=True)
    oc = o - mu
    var = jnp.mean(oc * oc, -1, keepdims=True)
    return oc * lax.rsqrt(var + LN_EPS)


def _log_gamma(h):
    return float(np.log(np.float32(1.0) - np.float32(2.0) ** np.float32(-5.0 - h), dtype=np.float32))


def _exp32(v):
    return float(np.exp(np.float32(v), dtype=np.float32))


def _whole(memory_space=pltpu.VMEM):
    return pl.BlockSpec(memory_space=memory_space)


def _params(n_axes):
    return pltpu.CompilerParams(dimension_semantics=("arbitrary",) * n_axes, vmem_limit_bytes=VMEM_LIMIT)


def _inproj_common(x, tab, g0, b0, w1, qg, kg, wuq):
    xn = _layer_norm(x, g0, b0)
    h = _dot(xn.astype(BF16), w1)
    o = 0
    q_lat = h[:, o:o + Q_LORA]; o += Q_LORA
    c_raw = h[:, o:o + KV_LORA]; o += KV_LORA
    rq = h[:, o:o + RET_W]; o += RET_W
    rk = h[:, o:o + RET_W]; o += RET_W
    rv = h[:, o:o + RET_W]; o += RET_W
    kr = h[:, o:o + LANES]; o += LANES
    krs = h[:, o:o + LANES]
    qc, qs1, qs2, rc, rs, kc, ks = [tab[:, i * LANES:(i + 1) * LANES] for i in range(N_TAB)]
    c = _rms_norm(c_raw, kg)
    krope = kr * kc + krs * ks
    q = _dot(_rms_norm(q_lat, qg).astype(BF16), wuq)
    q_heads = []
    for hh in range(MLA_HEADS):
        qh = q[:, hh * LANES:(hh + 1) * LANES]
        q_heads.append(qh * qc + pltpu.roll(qh, ROPE_DIM // 2, 1) * qs1
                       + pltpu.roll(qh, LANES - ROPE_DIM // 2, 1) * qs2)

    def rope_ret(x):
        return [x[:, hh * RET_DK:(hh + 1) * RET_DK] * rc
                + pltpu.roll(x[:, hh * RET_DK:(hh + 1) * RET_DK], RET_DK // 2, 1) * rs
                for hh in range(RET_HEADS)]

    rq_h = rope_ret(rq)
    rk_h = [v * (RET_DK ** -0.5) for v in rope_ret(rk)]
    return q_heads, c, krope, rq_h, rk_h, rv


def _inproj_prompt_kernel(x_ref, meta_ref, tab_ref, g0, b0, w1, qg, kg, wuq, wukp, e8, wuv,
                          q_o, k_o, v_o, c_o, kr_o, rq_o, rk_o, rv_o):
    t = pl.program_id(1)
    x = jnp.where(t == 0, meta_ref[...], x_ref[0])
    q_heads, c, krope, rq_h, rk_h, rv = _inproj_common(
        x, tab_ref[...], g0[...], b0[...], w1[...], qg[...], kg[...], wuq[...])
    row = lax.broadcasted_iota(jnp.int32, (TILE, 1), 0)
    real = (row >= FRONT_PAD) | (t > 0)
    for hh in range(MLA_HEADS):
        q_o[0, :, hh * LANES:(hh + 1) * LANES] = q_heads[hh].astype(BF16)
    for hh in range(RET_HEADS):
        sl = slice(hh * RET_DK, (hh + 1) * RET_DK)
        rq_o[0, :, sl] = rq_h[hh]
        rk_o[0, :, sl] = jnp.where(real, rk_h[hh], 0.0)
    rv_o[0] = rv
    cb = c.astype(BF16)
    k_o[0] = (_dot(cb, wukp[...]) + _dot(krope.astype(BF16), e8[...])).astype(BF16)
    v_o[0] = _dot(cb, wuv[...]).astype(BF16)
    c_o[0] = c
    kr_o[0] = krope[:, :ROPE_DIM]


def _inproj_sample_kernel(x_ref, tab_ref, g0, b0, w1, qg, kg, wuq, wukt,
                          q_o, qabs_o, c_o, kr_o, rq_o, rk_o, rv_o):
    q_heads, c, krope, rq_h, rk_h, rv = _inproj_common(
        x_ref[...], tab_ref[...], g0[...], b0[...], w1[...], qg[...], kg[...], wuq[...])
    for hh in range(MLA_HEADS):
        qb = q_heads[hh].astype(BF16)
        q_o[:, hh * LANES:(hh + 1) * LANES] = qb
        qabs_o[hh] = _dot(qb, wukt[hh]).astype(BF16)
    for hh in range(RET_HEADS):
        sl = slice(hh * RET_DK, (hh + 1) * RET_DK)
        rq_o[:, sl] = rq_h[hh]
        rk_o[:, sl] = rk_h[hh]
    rv_o[...] = rv
    c_o[...] = c
    kr_o[...] = krope[:, :ROPE_DIM]


def _attn_prompt_kernel(q_ref, k_ref, v_ref, o_ref):
    qi = pl.program_id(2)
    q = q_ref[0]
    q_pair = (q[:, :LANES], q[:, LANES:])
    row = lax.broadcasted_iota(jnp.int32, (TILE, TILE), 0) + qi * TILE
    lane = lax.broadcasted_iota(jnp.int32, (TILE, TILE), 1)
    first = lane < V_DIM

    def body(j, carry):
        m_pair, l_pair, acc = carry
        off = pl.multiple_of(j * TILE, TILE)
        kblk = k_ref[0, pl.ds(off, TILE), :]
        vblk = v_ref[0, pl.ds(off, TILE), :]
        col = lane + j * TILE
        valid = jnp.where(col >= FRONT_PAD, col, jnp.int32(1 << 30)) <= row
        new_m, new_l, alphas, pvs = [], [], [], []
        for hh in range(2):
            s = _dot_nt(q_pair[hh], kblk[:, hh * LANES:(hh + 1) * LANES])
            s = jnp.where(valid, s, NEG)
            m_new = jnp.maximum(m_pair[hh], jnp.max(s, -1, keepdims=True))
            p = jnp.exp(s - m_new)
            alpha = jnp.exp(m_pair[hh] - m_new)
            new_m.append(m_new)
            new_l.append(alpha * l_pair[hh] + jnp.sum(p, -1, keepdims=True))
            alphas.append(alpha)
            pvs.append(_dot(p.astype(BF16), vblk))
        acc = jnp.where(first, alphas[0], alphas[1]) * acc + jnp.where(first, pvs[0], pvs[1])
        return tuple(new_m), tuple(new_l), acc

    m0 = jnp.full((TILE, 1), NEG, F32)
    l0 = jnp.zeros((TILE, 1), F32)
    _, l_pair, acc = lax.fori_loop(0, qi + 1, body, ((m0, m0), (l0, l0), jnp.zeros((TILE, TILE), F32)))
    o_ref[0] = (acc / jnp.where(first, l_pair[0], l_pair[1])).astype(BF16)


def _ret_prompt_kernel(rq_ref, rk_ref, rv_ref, gn_ref, o_ref, st_ref, state):
    t = pl.program_id(1)

    @pl.when(t == 0)
    def _():
        state[...] = jnp.zeros_like(state)

    ii = lax.broadcasted_iota(jnp.int32, (TILE, TILE), 0).astype(F32)
    jj = lax.broadcasted_iota(jnp.int32, (TILE, TILE), 1).astype(F32)
    diff = ii - jj
    idx = lax.broadcasted_iota(jnp.int32, (TILE, 1), 0).astype(F32)
    for hh in range(RET_HEADS):
        sl = slice(hh * RET_DK, (hh + 1) * RET_DK)
        lg = _log_gamma(hh)
        decay = jnp.where(diff >= 0, jnp.exp(jnp.maximum(diff, 0.0) * lg), 0.0)
        q = rq_ref[0, :, sl]
        k = rk_ref[0, :, sl]
        vb = rv_ref[0, :, sl].astype(BF16)
        st = state[hh]
        inner = _dot_nt(q.astype(BF16), k.astype(BF16)) * decay
        o = _dot(inner.astype(BF16), vb) + _dot((q * jnp.exp((idx + 1.0) * lg)).astype(BF16), st.astype(BF16))
        k_dec = k * jnp.exp((TILE - 1.0 - idx) * lg)
        state[hh] = _exp32(TILE * lg) * st + _dot(k_dec.T.astype(BF16), vb)
        o_ref[0, :, sl] = _group_norm(o) * gn_ref[:, sl]

    @pl.when(t == pl.num_programs(1) - 1)
    def _():
        st_ref[0] = state[...]


def _merge(xn, attn_proj, ron, w2, wro, wo, g1, b1):
    g3 = _dot(xn.astype(BF16), w2)
    rg = g3[:, :RET_W]
    ga = g3[:, RET_W:RET_W + D_MODEL]
    gb = g3[:, RET_W + D_MODEL:]
    ret = (ron * (rg * jax.nn.sigmoid(rg))).astype(BF16)
    m = jax.nn.sigmoid(ga) * attn_proj + jax.nn.sigmoid(gb) * _dot(ret, wro)
    return _layer_norm(ALPHA * xn + _dot(m.astype(BF16), wo), g1, b1)


def _ffn_tail(x1, a, a1, a2, g, cw, cb, wfo, g2, b2):
    acc = cb + cw[0:1] * a2 + cw[1:2] * a1 + cw[2:3] * a
    hid = (jax.nn.gelu(acc) * g).astype(BF16)
    return _layer_norm(ALPHA * x1 + _dot(hid, wfo), g2, b2)


def _merge_ffn_prompt_kernel(x_ref, meta_ref, ao_ref, ro_ref, g0, b0, w2, wao, wro, wo, g1, b1,
                             wfi, cw, cb, wfo, g2, b2, y_ref, cs_ref, carry):
    t = pl.program_id(1)
    x = jnp.where(t == 0, meta_ref[...], x_ref[0])
    xn = _layer_norm(x, g0[...], b0[...])
    x1 = _merge(xn, _dot(ao_ref[0], wao[...]), ro_ref[0], w2[...], wro[...], wo[...], g1[...], b1[...])
    x1b = x1.astype(BF16)
    a = _dot(x1b, wfi[:, :D_FF])
    g = _dot(x1b, wfi[:, D_FF:])

    @pl.when(t == 0)
    def _():
        carry[...] = jnp.zeros_like(carry)

    prev = carry[...]
    row = lax.broadcasted_iota(jnp.int32, (TILE, 1), 0)
    a1 = jnp.where(row == 0, prev[7:8], pltpu.roll(a, 1, 0))
    a2 = jnp.where(row == 0, prev[6:7], jnp.where(row == 1, prev[7:8], pltpu.roll(a, 2, 0)))
    a1 = jnp.where((t > 0) | (row >= FRONT_PAD + 1), a1, 0.0)
    a2 = jnp.where((t > 0) | (row >= FRONT_PAD + 2), a2, 0.0)
    carry[...] = a[TILE - 8:]

    @pl.when(t > 0)
    def _():
        y_ref[0] = _ffn_tail(x1, a, a1, a2, g, cw[...], cb[...], wfo[...], g2[...], b2[...])

    @pl.when(t == pl.num_programs(1) - 1)
    def _():
        cs_ref[0] = a[TILE - (CONV_W - 1):]


def _merge_ffn_sample_kernel(x_ref, olat_ref, ro_ref, prev_ref, g0, b0, w2, wuv3, wao, wro, wo, g1, b1,
                             wfi, cw, cb, wfo, g2, b2, y_ref, cs_ref):
    xn = _layer_norm(x_ref[...], g0[...], b0[...])
    attn_proj = None
    for hh in range(MLA_HEADS):
        oh = _dot(olat_ref[hh].astype(BF16), wuv3[hh]).astype(BF16)
        part = _dot(oh, wao[hh * V_DIM:(hh + 1) * V_DIM, :])
        attn_proj = part if attn_proj is None else attn_proj + part
    x1 = _merge(xn, attn_proj, ro_ref[...], w2[...], wro[...], wo[...], g1[...], b1[...])
    x1b = x1.astype(BF16)
    a = _dot(x1b, wfi[:, :D_FF])
    g = _dot(x1b, wfi[:, D_FF:])
    prev0 = prev_ref[:, :D_FF]
    prev1 = prev_ref[:, D_FF:]
    y_ref[...] = _ffn_tail(x1, a, prev1, prev0, g, cw[...], cb[...], wfo[...], g2[...], b2[...])
    cs_ref[:, :D_FF] = prev1
    cs_ref[:, D_FF:] = a


def _attn_sample_kernel(pt_ref, qabs_ref, qr_ref, cnew_ref, krnew_ref, cache_c, cache_kr, o_ref,
                        cbuf, krbuf, sems, *, n_chunks):
    b = pl.program_id(0)
    nb = pl.num_programs(0)
    chunk_rows = PAGES_PER_CHUNK * PAGE_SIZE

    def copies(bb, ci, slot):
        out = []
        for p in range(PAGES_PER_CHUNK):
            page = pt_ref[bb, ci * PAGES_PER_CHUNK + p]
            rows = pl.ds(p * PAGE_SIZE, PAGE_SIZE)
            out.append(pltpu.make_async_copy(cache_c.at[page], cbuf.at[slot, rows], sems.at[0, slot]))
            out.append(pltpu.make_async_copy(cache_kr.at[page], krbuf.at[slot, rows], sems.at[1, slot]))
        return out

    @pl.when(b == 0)
    def _():
        for cp in copies(0, 0, 0):
            cp.start()

    qabs = qabs_ref[0]
    qr = qr_ref[0]
    m = jnp.full((MLA_HEADS, 1), NEG, F32)
    l = jnp.zeros((MLA_HEADS, 1), F32)
    acc = jnp.zeros((MLA_HEADS, KV_LORA), F32)
    for ci in range(n_chunks):
        slot = ci % 2
        if ci + 1 < n_chunks:
            for cp in copies(b, ci + 1, 1 - slot):
                cp.start()
        else:
            @pl.when(b + 1 < nb)
            def _():
                for cp in copies(b + 1, 0, 1 - slot):
                    cp.start()
        for cp in copies(b, ci, slot):
            cp.wait()
        cb = cbuf[slot].astype(BF16)
        krb = krbuf[slot].astype(BF16)
        s = _dot_nt(qabs, cb) + _dot_nt(qr, krb)
        m_new = jnp.maximum(m, jnp.max(s, -1, keepdims=True))
        p = jnp.exp(s - m_new)
        alpha = jnp.exp(m - m_new)
        l = alpha * l + jnp.sum(p, -1, keepdims=True)
        acc = alpha * acc + _dot(p.astype(BF16), cb)
        m = m_new
    cnew = cnew_ref[0]
    s_new = (jnp.sum(qabs.astype(F32) * cnew, -1, keepdims=True)
             + jnp.sum(qr.astype(F32) * krnew_ref[0], -1, keepdims=True))
    m_new = jnp.maximum(m, s_new)
    p_new = jnp.exp(s_new - m_new)
    alpha = jnp.exp(m - m_new)
    o_ref[0] = (alpha * acc + p_new * cnew) / (alpha * l + p_new)


RS_ROWS = 8


def _ret_sample_kernel(rq_ref, rk_ref, rv_ref, st_ref, gn_ref, o_ref, ns_ref):
    for hh in range(RET_HEADS):
        sl = slice(hh * RET_DK, (hh + 1) * RET_DK)
        gamma = _exp32(_log_gamma(hh))
        q8 = rq_ref[:, sl]
        k8 = rk_ref[:, sl]
        v8 = rv_ref[:, sl]
        qk = jnp.sum(q8 * k8, -1, keepdims=True)
        for r in range(RS_ROWS):
            kcol = jnp.broadcast_to(k8[r:r + 1], (RET_DK, RET_DK)).T
            qcol = jnp.broadcast_to(q8[r:r + 1], (RET_DK, RET_DK)).T
            st = st_ref[r, hh]
            v = v8[r:r + 1]
            ns_ref[r, hh] = gamma * st + kcol * v
            o = qk[r:r + 1] * v + jnp.sum(qcol * gamma * st, 0, keepdims=True)
            o_ref[r:r + 1, sl] = _group_norm(o) * gn_ref[:, sl]


def _rope_tables(pos):
    n = pos.shape[0]

    def cos_sin(half):
        inv = ROPE_THETA ** (-jnp.arange(half, dtype=F32) / half)
        ang = pos[:, None] * inv[None, :]
        return jnp.cos(ang), jnp.sin(ang)

    c16, s16 = cos_sin(ROPE_DIM // 2)
    c64, s64 = cos_sin(RET_DK // 2)
    z = lambda w: jnp.zeros((n, w), F32)
    scale = (NOPE_DIM + ROPE_DIM) ** -0.5
    tail = LANES - NOPE_DIM - ROPE_DIM
    qc = jnp.concatenate([jnp.ones((n, NOPE_DIM), F32), c16, c16, z(tail)], 1) * scale
    qs1 = jnp.concatenate([z(NOPE_DIM + ROPE_DIM // 2), s16, z(tail)], 1) * scale
    qs2 = jnp.concatenate([z(NOPE_DIM), -s16, z(ROPE_DIM // 2 + tail)], 1) * scale
    rc = jnp.concatenate([c64, c64], 1)
    rs = jnp.concatenate([-s64, s64], 1)
    kc = jnp.concatenate([c16, c16, z(LANES - ROPE_DIM)], 1)
    ks = jnp.concatenate([-s16, s16, z(LANES - ROPE_DIM)], 1)
    return jnp.concatenate([qc, qs1, qs2, rc, rs, kc, ks], 1)


def _prep_weights(w_in, w_uq, w_uk, w_uv, w_ffn_in):
    sizes = (Q_LORA, KV_LORA, ROPE_DIM, RET_W, RET_W, RET_W, RET_W, D_MODEL, D_MODEL)
    offs = [0]
    for s in sizes:
        offs.append(offs[-1] + s)
    col = lambda i: w_in[:, offs[i]:offs[i + 1]]
    w_kr = col(2)
    half = ROPE_DIM // 2
    zpad = jnp.zeros((D_MODEL, LANES - ROPE_DIM), F32)
    w1 = jnp.concatenate([col(0), col(1), col(3), col(4), col(5), w_kr, zpad,
                          w_kr[:, half:], w_kr[:, :half], zpad], 1).astype(BF16)
    w2 = jnp.concatenate([col(6), col(7), col(8)], 1).astype(BF16)
    hd = NOPE_DIM + ROPE_DIM
    wuq = jnp.pad(w_uq, ((0, 0), (0, 0), (0, LANES - hd))).reshape(Q_LORA, HEAD_W).astype(BF16)
    wukp = jnp.pad(w_uk, ((0, 0), (0, 0), (0, LANES - NOPE_DIM))).reshape(KV_LORA, HEAD_W).astype(BF16)
    e1 = jnp.pad(jnp.eye(ROPE_DIM, dtype=F32), ((0, LANES - ROPE_DIM), (NOPE_DIM, LANES - hd)))
    e8 = jnp.tile(e1, (1, MLA_HEADS)).astype(BF16)
    wuv = w_uv.reshape(KV_LORA, MLA_HEADS * V_DIM).astype(BF16)
    wukt = jnp.pad(jnp.transpose(w_uk, (1, 2, 0)), ((0, 0), (0, LANES - NOPE_DIM), (0, 0))).astype(BF16)
    wuv3 = jnp.transpose(w_uv, (1, 0, 2)).astype(BF16)
    return w1, w2, wuq, wukp, e8, wuv, wukt, wuv3, w_ffn_in.astype(BF16)


def kernel(x_prompt, x_sample, cache_kv_latent, cache_k_rope, state_retention, state_ffn_conv, page_table, meta_tokens, ln0_g, ln0_b, w_in, q_norm_g, kv_norm_g, w_uq, w_uk, w_uv, ret_gn_g, w_attn_out, w_ret_out, w_o, ln1_g, ln1_b, w_ffn_in, conv_w, conv_b, w_ffn_out, ln2_g, ln2_b):
    assert w_in.shape[0] == DEPTH == 1
    nb, seq, _ = x_prompt.shape
    db, dec_seq, _ = x_sample.shape
    n_pages = page_table.shape[1]
    assert seq % TILE == 0 and dec_seq == 1 and db % RS_ROWS == 0 and n_pages % (2 * PAGES_PER_CHUNK) == 0
    nt = seq // TILE + 1
    plen = nt * TILE
    past = n_pages * PAGE_SIZE

    w1, w2, wuq, wukp, e8, wuv, wukt, wuv3, wfi = _prep_weights(w_in[0], w_uq[0], w_uk[0], w_uv[0], w_ffn_in[0])
    wao = w_attn_out[0].astype(BF16)
    wro = w_ret_out[0].astype(BF16)
    wo = w_o[0].astype(BF16)
    wfo = w_ffn_out[0].astype(BF16)
    row = lambda v: v.reshape(1, -1)
    g0, b0 = row(ln0_g), row(ln0_b)
    g1, b1, g2, b2 = row(ln1_g[0]), row(ln1_b[0]), row(ln2_g[0]), row(ln2_b[0])
    qg, kg, gn = row(q_norm_g[0]), row(kv_norm_g[0]), row(ret_gn_g[0])
    cw, cb = conv_w[0], row(conv_b[0])

    meta_tile = jnp.concatenate([jnp.zeros((FRONT_PAD, D_MODEL), F32), meta_tokens.astype(F32)], 0)
    tab_p = _rope_tables(jnp.maximum(jnp.arange(plen, dtype=F32) - FRONT_PAD, 0.0))
    tab_s = jnp.broadcast_to(_rope_tables(jnp.full((1,), past, F32)), (db, N_TAB * LANES))

    tok = lambda w: pl.BlockSpec((1, TILE, w), lambda b, t: (b, t, 0))
    x_spec = pl.BlockSpec((1, TILE, D_MODEL), lambda b, t: (b, jnp.maximum(t - 1, 0), 0))
    wspec = _whole()
    sds = jax.ShapeDtypeStruct

    q_p, k_p, v_p, c_p, kr_p, rq_p, rk_p, rv_p = pl.pallas_call(
        _inproj_prompt_kernel,
        grid=(nb, nt),
        in_specs=[x_spec, wspec, pl.BlockSpec((TILE, N_TAB * LANES), lambda b, t: (t, 0))] + [wspec] * 9,
        out_specs=[tok(HEAD_W), tok(HEAD_W), tok(MLA_HEADS * V_DIM), tok(KV_LORA), tok(ROPE_DIM),
                   tok(RET_W), tok(RET_W), tok(RET_W)],
        out_shape=[sds((nb, plen, HEAD_W), BF16), sds((nb, plen, HEAD_W), BF16),
                   sds((nb, plen, MLA_HEADS * V_DIM), BF16), sds((nb, plen, KV_LORA), F32),
                   sds((nb, plen, ROPE_DIM), F32), sds((nb, plen, RET_W), F32),
                   sds((nb, plen, RET_W), F32), sds((nb, plen, RET_W), F32)],
        compiler_params=_params(2),
        name="inproj_prompt",
    )(x_prompt, meta_tile, tab_p, g0, b0, w1, qg, kg, wuq, wukp, e8, wuv)

    n_pairs = MLA_HEADS // 2
    ao_p = pl.pallas_call(
        _attn_prompt_kernel,
        grid=(nb, n_pairs, nt),
        in_specs=[pl.BlockSpec((1, TILE, 2 * LANES), lambda b, h, t: (b, t, h)),
                  pl.BlockSpec((1, plen, 2 * LANES), lambda b, h, t: (b, 0, h)),
                  pl.BlockSpec((1, plen, 2 * V_DIM), lambda b, h, t: (b, 0, h))],
        out_specs=pl.BlockSpec((1, TILE, 2 * V_DIM), lambda b, h, t: (b, t, h)),
        out_shape=sds((nb, plen, MLA_HEADS * V_DIM), BF16),
        compiler_params=_params(3),
        name="attn_prompt",
    )(q_p, k_p, v_p)

    ro_p, st_p = pl.pallas_call(
        _ret_prompt_kernel,
        grid=(nb, nt),
        in_specs=[tok(RET_W), tok(RET_W), tok(RET_W), wspec],
        out_specs=[tok(RET_W), pl.BlockSpec((1, RET_HEADS, RET_DK, RET_DV), lambda b, t: (b, 0, 0, 0))],
        out_shape=[sds((nb, plen, RET_W), F32), sds((nb, RET_HEADS, RET_DK, RET_DV), F32)],
        scratch_shapes=[pltpu.VMEM((RET_HEADS, RET_DK, RET_DV), F32)],
        compiler_params=_params(2),
        name="ret_prompt",
    )(rq_p, rk_p, rv_p, gn)

    y_p, cs_p = pl.pallas_call(
        _merge_ffn_prompt_kernel,
        grid=(nb, nt),
        in_specs=[x_spec, wspec, tok(MLA_HEADS * V_DIM), tok(RET_W)] + [wspec] * 14,
        out_specs=[pl.BlockSpec((1, TILE, D_MODEL), lambda b, t: (b, jnp.maximum(t - 1, 0), 0)),
                   pl.BlockSpec((1, CONV_W - 1, D_FF), lambda b, t: (b, 0, 0))],
        out_shape=[sds((nb, seq, D_MODEL), F32), sds((nb, CONV_W - 1, D_FF), F32)],
        scratch_shapes=[pltpu.VMEM((8, D_FF), F32)],
        compiler_params=_params(2),
        name="merge_ffn_prompt",
    )(x_prompt, meta_tile, ao_p, ro_p, g0, b0, w2, wao, wro, wo, g1, b1, wfi, cw, cb, wfo, g2, b2)

    xs = x_sample.reshape(db, D_MODEL)
    q_s, qabs_s, c_s, kr_s, rq_s, rk_s, rv_s = pl.pallas_call(
        _inproj_sample_kernel,
        in_specs=[wspec] * 9,
        out_specs=[wspec] * 7,
        out_shape=[sds((db, HEAD_W), BF16), sds((MLA_HEADS, db, KV_LORA), BF16), sds((db, KV_LORA), F32),
                   sds((db, ROPE_DIM), F32), sds((db, RET_W), F32), sds((db, RET_W), F32), sds((db, RET_W), F32)],
        compiler_params=pltpu.CompilerParams(vmem_limit_bytes=VMEM_LIMIT),
        name="inproj_sample",
    )(xs, tab_s, g0, b0, w1, qg, kg, wuq, wukt)

    qabs_t = jnp.transpose(qabs_s, (1, 0, 2))
    qr_s = q_s.reshape(db, MLA_HEADS, LANES)[:, :, NOPE_DIM:NOPE_DIM + ROPE_DIM]
    n_chunks = n_pages // PAGES_PER_CHUNK
    chunk_rows = PAGES_PER_CHUNK * PAGE_SIZE
    per_req = lambda s1, s2: pl.BlockSpec((1, s1, s2), lambda b, pt: (b, 0, 0))
    olat_s = pl.pallas_call(
        functools.partial(_attn_sample_kernel, n_chunks=n_chunks),
        grid_spec=pltpu.PrefetchScalarGridSpec(
            num_scalar_prefetch=1,
            grid=(db,),
            in_specs=[per_req(MLA_HEADS, KV_LORA), per_req(MLA_HEADS, ROPE_DIM), per_req(1, KV_LORA),
                      per_req(1, ROPE_DIM), _whole(pl.ANY), _whole(pl.ANY)],
            out_specs=per_req(MLA_HEADS, KV_LORA),
            scratch_shapes=[pltpu.VMEM((2, chunk_rows, KV_LORA), F32),
                            pltpu.VMEM((2, chunk_rows, ROPE_DIM), F32),
                            pltpu.SemaphoreType.DMA((2, 2))]),
        out_shape=sds((db, MLA_HEADS, KV_LORA), F32),
        compiler_params=_params(1),
        name="attn_sample",
    )(page_table, qabs_t, qr_s, c_s.reshape(db, 1, KV_LORA), kr_s.reshape(db, 1, ROPE_DIM),
      cache_kv_latent[0], cache_k_rope[0])

    rows = lambda w: pl.BlockSpec((RS_ROWS, w), lambda i: (i, 0))
    st_spec = pl.BlockSpec((RS_ROWS, RET_HEADS, RET_DK, RET_DV), lambda i: (i, 0, 0, 0))
    ro_s, st_s = pl.pallas_call(
        _ret_sample_kernel,
        grid=(db // RS_ROWS,),
        in_specs=[rows(RET_W), rows(RET_W), rows(RET_W), st_spec, wspec],
        out_specs=[rows(RET_W), st_spec],
        out_shape=[sds((db, RET_W), F32), sds((db, RET_HEADS, RET_DK, RET_DV), F32)],
        compiler_params=_params(1),
        name="ret_sample",
    )(rq_s, rk_s, rv_s, state_retention[0], gn)

    y_s, cs_s = pl.pallas_call(
        _merge_ffn_sample_kernel,
        in_specs=[wspec] * 19,
        out_specs=[wspec] * 2,
        out_shape=[sds((db, D_MODEL), F32), sds((db, (CONV_W - 1) * D_FF), F32)],
        compiler_params=pltpu.CompilerParams(vmem_limit_bytes=VMEM_LIMIT),
        name="merge_ffn_sample",
    )(xs, jnp.transpose(olat_s, (1, 0, 2)), ro_s, state_ffn_conv[0].reshape(db, (CONV_W - 1) * D_FF),
      g0, b0, w2, wuv3, wao, wro, wo, g1, b1, wfi, cw, cb, wfo, g2, b2)

    first = FRONT_PAD
    return (y_p,
            y_s.reshape(db, 1, D_MODEL),
            c_p[None, :, first:],
            kr_p[None, :, first:],
            st_p[None],
            cs_p[None],
            c_s.reshape(1, db, 1, KV_LORA),
            kr_s.reshape(1, db, 1, ROPE_DIM),
            st_s[None],
            cs_s.reshape(1, db, CONV_W - 1, D_FF))
```

```python
import functools

import numpy as np
import jax
import jax.numpy as jnp
from jax import lax
from jax.experimental import pallas as pl
from jax.experimental.pallas import tpu as pltpu

F32 = jnp.float32
BF16 = jnp.bfloat16

D_MODEL = 1024
N_META = 16
MLA_HEADS = 8
Q_LORA = 384
KV_LORA = 256
NOPE_DIM = 64
ROPE_DIM = 32
V_DIM = 64
ROPE_THETA = 10000.0
RET_HEADS = 4
RET_DK = 128
RET_DV = 128
D_FF = 2816
CONV_W = 3
PAGE_SIZE = 128
LN_EPS = 1e-5
RMS_EPS = 1e-6
DEPTH = 1
ALPHA = (2 * DEPTH) ** 0.25

LANES = 128
TILE = 128
FRONT_PAD = TILE - N_META
RET_W = RET_HEADS * RET_DK
HEAD_W = MLA_HEADS * LANES
N_PAIRS = MLA_HEADS // 2
ATTN_CLASSES = 4
N_TAB = 7
W1_COLS = Q_LORA + KV_LORA + 3 * RET_W + 2 * LANES
W2_COLS = RET_W + 2 * D_MODEL
NEG = -1e30
PAGES_PER_CHUNK = 16
VMEM_LIMIT = 56 * 1024 * 1024


def _dot(a, b):
    return jnp.dot(a, b, preferred_element_type=F32)


def _dot_nt(a, b):
    return lax.dot_general(a, b, (((1,), (1,)), ((), ())), preferred_element_type=F32)


def _layer_norm(x, g, b):
    mu = jnp.mean(x, -1, keepdims=True)
    xc = x - mu
    var = jnp.mean(xc * xc, -1, keepdims=True)
    return xc * lax.rsqrt(var + LN_EPS) * g + b


def _rms_norm(x, g):
    return x * lax.rsqrt(jnp.mean(x * x, -1, keepdims=True) + RMS_EPS) * g


def _group_norm(o):
    mu = jnp.mean(o, -1, keepdims=True)
    oc = o - mu
    var = jnp.mean(oc * oc, -1, keepdims=True)
    return oc * lax.rsqrt(var + LN_EPS)


def _log_gamma(h):
    return float(np.log(np.float32(1.0) - np.float32(2.0) ** np.float32(-5.0 - h), dtype=np.float32))


def _exp32(v):
    return float(np.exp(np.float32(v), dtype=np.float32))


def _whole(memory_space=pltpu.VMEM):
    return pl.BlockSpec(memory_space=memory_space)


def _params(n_axes):
    return pltpu.CompilerParams(dimension_semantics=("arbitrary",) * n_axes, vmem_limit_bytes=VMEM_LIMIT)


def _inproj_common(x, tab, g0, b0, w1, qg, kg, wuq):
    xn = _layer_norm(x, g0, b0)
    h = _dot(xn.astype(BF16), w1)
    o = 0
    q_lat = h[:, o:o + Q_LORA]; o += Q_LORA
    c_raw = h[:, o:o + KV_LORA]; o += KV_LORA
    rq = h[:, o:o + RET_W]; o += RET_W
    rk = h[:, o:o + RET_W]; o += RET_W
    rv = h[:, o:o + RET_W]; o += RET_W
    kr = h[:, o:o + LANES]; o += LANES
    krs = h[:, o:o + LANES]
    qc, qs1, qs2, rc, rs, kc, ks = [tab[:, i * LANES:(i + 1) * LANES] for i in range(N_TAB)]
    c = _rms_norm(c_raw, kg)
    krope = kr * kc + krs * ks
    q = _dot(_rms_norm(q_lat, qg).astype(BF16), wuq)
    q_heads = []
    for hh in range(MLA_HEADS):
        qh = q[:, hh * LANES:(hh + 1) * LANES]
        q_heads.append(qh * qc + pltpu.roll(qh, ROPE_DIM // 2, 1) * qs1
                       + pltpu.roll(qh, LANES - ROPE_DIM // 2, 1) * qs2)

    def rope_ret(x):
        return [x[:, hh * RET_DK:(hh + 1) * RET_DK] * rc
                + pltpu.roll(x[:, hh * RET_DK:(hh + 1) * RET_DK], RET_DK // 2, 1) * rs
                for hh in range(RET_HEADS)]

    rq_h = rope_ret(rq)
    rk_h = [v * (RET_DK ** -0.5) for v in rope_ret(rk)]
    return q_heads, c, krope, rq_h, rk_h, rv


def _inproj_prompt_kernel(x_ref, meta_ref, tab_ref, g0, b0, w1, qg, kg, wuq, wukp, e8, wuv,
                          q_o, k_o, v_o, c_o, kr_o, rq_o, rk_o, rv_o):
    t = pl.program_id(1)
    x = jnp.where(t == 0, meta_ref[...], x_ref[0])
    q_heads, c, krope, rq_h, rk_h, rv = _inproj_common(
        x, tab_ref[...], g0[...], b0[...], w1[...], qg[...], kg[...], wuq[...])
    row = lax.broadcasted_iota(jnp.int32, (TILE, 1), 0)
    real = (row >= FRONT_PAD) | (t > 0)
    for hh in range(MLA_HEADS):
        q_o[0, hh // 2, :, (hh % 2) * LANES:(hh % 2 + 1) * LANES] = q_heads[hh].astype(BF16)
    for hh in range(RET_HEADS):
        sl = slice(hh * RET_DK, (hh + 1) * RET_DK)
        rq_o[0, :, sl] = rq_h[hh]
        rk_o[0, :, sl] = jnp.where(real, rk_h[hh], 0.0)
    rv_o[0] = rv
    cb = c.astype(BF16)
    k = (_dot(cb, wukp[...]) + _dot(krope.astype(BF16), e8[...])).astype(BF16)
    v = _dot(cb, wuv[...]).astype(BF16)
    for pr in range(N_PAIRS):
        k_o[0, pr] = k[:, pr * 2 * LANES:(pr + 1) * 2 * LANES]
        v_o[0, pr] = v[:, pr * 2 * V_DIM:(pr + 1) * 2 * V_DIM]
    c_o[0] = c
    kr_o[0] = krope[:, :ROPE_DIM]


def _inproj_sample_kernel(x_ref, tab_ref, g0, b0, w1, qg, kg, wuq, wukt,
                          q_o, qabs_o, c_o, kr_o, rq_o, rk_o, rv_o):
    q_heads, c, krope, rq_h, rk_h, rv = _inproj_common(
        x_ref[...], tab_ref[...], g0[...], b0[...], w1[...], qg[...], kg[...], wuq[...])
    for hh in range(MLA_HEADS):
        qb = q_heads[hh].astype(BF16)
        q_o[:, hh * LANES:(hh + 1) * LANES] = qb
        qabs_o[hh] = _dot(qb, wukt[hh]).astype(BF16)
    for hh in range(RET_HEADS):
        sl = slice(hh * RET_DK, (hh + 1) * RET_DK)
        rq_o[:, sl] = rq_h[hh]
        rk_o[:, sl] = rk_h[hh]
    rv_o[...] = rv
    c_o[...] = c
    kr_o[...] = krope[:, :ROPE_DIM]


def _attn_prompt_kernel(q_ref, k_ref, v_ref, o_ref, *, bounds):
    qi = pl.program_id(1)
    row = lax.broadcasted_iota(jnp.int32, (TILE, TILE), 0) + qi * TILE
    lane = lax.broadcasted_iota(jnp.int32, (TILE, TILE), 1)
    first = lane < V_DIM

    def run(lo, wt):
        width = wt * TILE

        def pair_body(pr, carry):
            q = q_ref[0, pr]
            k = k_ref[0, pr, :width, :]
            v = v_ref[0, pr, :width, :]
            outs = []
            for hh in range(2):
                s = _dot_nt(q[:, hh * LANES:(hh + 1) * LANES], k[:, hh * LANES:(hh + 1) * LANES])
                tiles = []
                for c in range(wt):
                    sc = s[:, c * TILE:(c + 1) * TILE]
                    if c == 0:
                        sc = jnp.where(lane >= FRONT_PAD, sc, NEG)
                    if c >= lo:
                        sc = jnp.where(lane + c * TILE <= row, sc, NEG)
                    tiles.append(sc)
                s = jnp.concatenate(tiles, 1)
                p = jnp.exp(s - jnp.max(s, -1, keepdims=True))
                inv_l = 1.0 / jnp.sum(p, -1, keepdims=True)
                outs.append(_dot(p.astype(BF16), v) * inv_l)
            o_ref[0, pr] = jnp.where(first, outs[0], outs[1]).astype(BF16)
            return carry

        lax.fori_loop(0, N_PAIRS, pair_body, 0, unroll=True)

    lo = 0
    for wt in bounds:
        pl.when((qi >= lo) & (qi < wt))(functools.partial(run, lo, wt))
        lo = wt


def _ret_prompt_kernel(rq_ref, rk_ref, rv_ref, gn_ref, o_ref, st_ref, state):
    t = pl.program_id(1)

    @pl.when(t == 0)
    def _():
        state[...] = jnp.zeros_like(state)

    ii = lax.broadcasted_iota(jnp.int32, (TILE, TILE), 0).astype(F32)
    jj = lax.broadcasted_iota(jnp.int32, (TILE, TILE), 1).astype(F32)
    diff = ii - jj
    idx = lax.broadcasted_iota(jnp.int32, (TILE, 1), 0).astype(F32)
    for hh in range(RET_HEADS):
        sl = slice(hh * RET_DK, (hh + 1) * RET_DK)
        lg = _log_gamma(hh)
        decay = jnp.where(diff >= 0, jnp.exp(jnp.maximum(diff, 0.0) * lg), 0.0)
        q = rq_ref[0, :, sl]
        k = rk_ref[0, :, sl]
        vb = rv_ref[0, :, sl].astype(BF16)
        st = state[hh]
        inner = _dot_nt(q.astype(BF16), k.astype(BF16)) * decay
        o = _dot(inner.astype(BF16), vb) + _dot((q * jnp.exp((idx + 1.0) * lg)).astype(BF16), st.astype(BF16))
        k_dec = k * jnp.exp((TILE - 1.0 - idx) * lg)
        state[hh] = _exp32(TILE * lg) * st + _dot(k_dec.T.astype(BF16), vb)
        o_ref[0, :, sl] = _group_norm(o) * gn_ref[:, sl]

    @pl.when(t == pl.num_programs(1) - 1)
    def _():
        st_ref[0] = state[...]


def _merge(xn, attn_proj, ron, w2, wro, wo, g1, b1):
    g3 = _dot(xn.astype(BF16), w2)
    rg = g3[:, :RET_W]
    ga = g3[:, RET_W:RET_W + D_MODEL]
    gb = g3[:, RET_W + D_MODEL:]
    ret = (ron * (rg * jax.nn.sigmoid(rg))).astype(BF16)
    m = jax.nn.sigmoid(ga) * attn_proj + jax.nn.sigmoid(gb) * _dot(ret, wro)
    return _layer_norm(ALPHA * xn + _dot(m.astype(BF16), wo), g1, b1)


def _ffn_tail(x1, a, a1, a2, g, cw, cb, wfo, g2, b2):
    acc = cb + cw[0:1] * a2 + cw[1:2] * a1 + cw[2:3] * a
    hid = (jax.nn.gelu(acc) * g).astype(BF16)
    return _layer_norm(ALPHA * x1 + _dot(hid, wfo), g2, b2)


def _merge_ffn_prompt_kernel(x_ref, meta_ref, ao_ref, ro_ref, g0, b0, w2, wao, wro, wo, g1, b1,
                             wfi, cw, cb, wfo, g2, b2, y_ref, cs_ref, carry):
    t = pl.program_id(1)
    x = jnp.where(t == 0, meta_ref[...], x_ref[0])
    xn = _layer_norm(x, g0[...], b0[...])
    attn_proj = None
    for pr in range(N_PAIRS):
        part = _dot(ao_ref[0, pr], wao[pr * 2 * V_DIM:(pr + 1) * 2 * V_DIM, :])
        attn_proj = part if attn_proj is None else attn_proj + part
    x1 = _merge(xn, attn_proj, ro_ref[0], w2[...], wro[...], wo[...], g1[...], b1[...])
    x1b = x1.astype(BF16)
    a = _dot(x1b, wfi[:, :D_FF])
    g = _dot(x1b, wfi[:, D_FF:])

    @pl.when(t == 0)
    def _():
        carry[...] = jnp.zeros_like(carry)

    prev = carry[...]
    row = lax.broadcasted_iota(jnp.int32, (TILE, 1), 0)
    a1 = jnp.where(row == 0, prev[7:8], pltpu.roll(a, 1, 0))
    a2 = jnp.where(row == 0, prev[6:7], jnp.where(row == 1, prev[7:8], pltpu.roll(a, 2, 0)))
    a1 = jnp.where((t > 0) | (row >= FRONT_PAD + 1), a1, 0.0)
    a2 = jnp.where((t > 0) | (row >= FRONT_PAD + 2), a2, 0.0)
    carry[...] = a[TILE - 8:]

    @pl.when(t > 0)
    def _():
        y_ref[0] = _ffn_tail(x1, a, a1, a2, g, cw[...], cb[...], wfo[...], g2[...], b2[...])

    @pl.when(t == pl.num_programs(1) - 1)
    def _():
        cs_ref[0] = a[TILE - (CONV_W - 1):]


def _merge_ffn_sample_kernel(x_ref, olat_ref, ro_ref, prev_ref, g0, b0, w2, wuv3, wao, wro, wo, g1, b1,
                             wfi, cw, cb, wfo, g2, b2, y_ref, cs_ref):
    xn = _layer_norm(x_ref[...], g0[...], b0[...])
    attn_proj = None
    for hh in range(MLA_HEADS):
        oh = _dot(olat_ref[hh].astype(BF16), wuv3[hh]).astype(BF16)
        part = _dot(oh, wao[hh * V_DIM:(hh + 1) * V_DIM, :])
        attn_proj = part if attn_proj is None else attn_proj + part
    x1 = _merge(xn, attn_proj, ro_ref[...], w2[...], wro[...], wo[...], g1[...], b1[...])
    x1b = x1.astype(BF16)
    a = _dot(x1b, wfi[:, :D_FF])
    g = _dot(x1b, wfi[:, D_FF:])
    prev0 = prev_ref[:, :D_FF]
    prev1 = prev_ref[:, D_FF:]
    y_ref[...] = _ffn_tail(x1, a, prev1, prev0, g, cw[...], cb[...], wfo[...], g2[...], b2[...])
    cs_ref[:, :D_FF] = prev1
    cs_ref[:, D_FF:] = a


def _attn_sample_kernel(pt_ref, qabs_ref, qr_ref, cnew_ref, krnew_ref, cache_c, cache_kr, o_ref,
                        cbuf, krbuf, cb16, sems, *, n_chunks):
    b = pl.program_id(0)
    nb = pl.num_programs(0)

    def copies(bb, ci, slot):
        out = []
        for p in range(PAGES_PER_CHUNK):
            page = pt_ref[bb, ci * PAGES_PER_CHUNK + p]
            rows = pl.ds(p * PAGE_SIZE, PAGE_SIZE)
            out.append((pltpu.make_async_copy(cache_c.at[0, page], cbuf.at[slot, rows], sems.at[0, slot]), p % 2))
            out.append((pltpu.make_async_copy(cache_kr.at[0, page], krbuf.at[slot, rows], sems.at[1, slot]),
                        (p + 1) % 2))
        return out

    def start(bb, ci, slot):
        for cp, prio in copies(bb, ci, slot):
            cp.start(priority=prio)

    @pl.when(b == 0)
    def _():
        start(0, 0, 0)

    qabs = qabs_ref[0]
    qr = qr_ref[0]
    m = jnp.full((MLA_HEADS, 1), NEG, F32)
    l = jnp.zeros((MLA_HEADS, 1), F32)
    acc = jnp.zeros((MLA_HEADS, KV_LORA), F32)
    pending = None
    for ci in range(n_chunks):
        slot = ci % 2
        if ci + 1 < n_chunks:
            start(b, ci + 1, 1 - slot)
        else:
            pl.when(b + 1 < nb)(functools.partial(start, b + 1, 0, 1 - slot))
        for cp, _ in copies(b, ci, slot):
            cp.wait()
        cb16[slot] = cbuf[slot].astype(BF16)
        s = _dot_nt(qabs, cb16[slot]) + _dot_nt(qr, krbuf[slot].astype(BF16))
        if pending is not None:
            acc = pending[1] * acc + _dot(pending[0], cb16[pending[2]])
        m_new = jnp.maximum(m, jnp.max(s, -1, keepdims=True))
        p = jnp.exp(s - m_new)
        alpha = jnp.exp(m - m_new)
        l = alpha * l + jnp.sum(p, -1, keepdims=True)
        m = m_new
        pending = (p.astype(BF16), alpha, slot)
    acc = pending[1] * acc + _dot(pending[0], cb16[pending[2]])
    cnew = cnew_ref[0]
    s_new = (jnp.sum(qabs.astype(F32) * cnew, -1, keepdims=True)
             + jnp.sum(qr.astype(F32) * krnew_ref[0], -1, keepdims=True))
    m_new = jnp.maximum(m, s_new)
    p_new = jnp.exp(s_new - m_new)
    alpha = jnp.exp(m - m_new)
    o_ref[0] = (alpha * acc + p_new * cnew) / (alpha * l + p_new)


RS_ROWS = 8


def _ret_sample_kernel(rq_ref, rk_ref, rv_ref, st_ref, gn_ref, o_ref, ns_ref):
    for hh in range(RET_HEADS):
        sl = slice(hh * RET_DK, (hh + 1) * RET_DK)
        gamma = _exp32(_log_gamma(hh))
        q8 = rq_ref[:, sl]
        k8 = rk_ref[:, sl]
        v8 = rv_ref[:, sl]
        qk = jnp.sum(q8 * k8, -1, keepdims=True)
        for r in range(RS_ROWS):
            kcol = jnp.broadcast_to(k8[r:r + 1], (RET_DK, RET_DK)).T
            qcol = jnp.broadcast_to(q8[r:r + 1], (RET_DK, RET_DK)).T
            st = st_ref[r, hh]
            v = v8[r:r + 1]
            ns_ref[r, hh] = gamma * st + kcol * v
            o = qk[r:r + 1] * v + jnp.sum(qcol * gamma * st, 0, keepdims=True)
            o_ref[r:r + 1, sl] = _group_norm(o) * gn_ref[:, sl]


def _rope_tables(pos):
    n = pos.shape[0]

    def cos_sin(half):
        inv = ROPE_THETA ** (-jnp.arange(half, dtype=F32) / half)
        ang = pos[:, None] * inv[None, :]
        return jnp.cos(ang), jnp.sin(ang)

    c16, s16 = cos_sin(ROPE_DIM // 2)
    c64, s64 = cos_sin(RET_DK // 2)
    z = lambda w: jnp.zeros((n, w), F32)
    scale = (NOPE_DIM + ROPE_DIM) ** -0.5
    tail = LANES - NOPE_DIM - ROPE_DIM
    qc = jnp.concatenate([jnp.ones((n, NOPE_DIM), F32), c16, c16, z(tail)], 1) * scale
    qs1 = jnp.concatenate([z(NOPE_DIM + ROPE_DIM // 2), s16, z(tail)], 1) * scale
    qs2 = jnp.concatenate([z(NOPE_DIM), -s16, z(ROPE_DIM // 2 + tail)], 1) * scale
    rc = jnp.concatenate([c64, c64], 1)
    rs = jnp.concatenate([-s64, s64], 1)
    kc = jnp.concatenate([c16, c16, z(LANES - ROPE_DIM)], 1)
    ks = jnp.concatenate([-s16, s16, z(LANES - ROPE_DIM)], 1)
    return jnp.concatenate([qc, qs1, qs2, rc, rs, kc, ks], 1)


def _prep_weights(w_in, w_uq, w_uk, w_uv, w_ffn_in):
    sizes = (Q_LORA, KV_LORA, ROPE_DIM, RET_W, RET_W, RET_W, RET_W, D_MODEL, D_MODEL)
    offs = [0]
    for s in sizes:
        offs.append(offs[-1] + s)
    col = lambda i: w_in[:, offs[i]:offs[i + 1]]
    w_kr = col(2)
    half = ROPE_DIM // 2
    zpad = jnp.zeros((D_MODEL, LANES - ROPE_DIM), F32)
    w1 = jnp.concatenate([col(0), col(1), col(3), col(4), col(5), w_kr, zpad,
                          w_kr[:, half:], w_kr[:, :half], zpad], 1).astype(BF16)
    w2 = jnp.concatenate([col(6), col(7), col(8)], 1).astype(BF16)
    hd = NOPE_DIM + ROPE_DIM
    wuq = jnp.pad(w_uq, ((0, 0), (0, 0), (0, LANES - hd))).reshape(Q_LORA, HEAD_W).astype(BF16)
    wukp = jnp.pad(w_uk, ((0, 0), (0, 0), (0, LANES - NOPE_DIM))).reshape(KV_LORA, HEAD_W).astype(BF16)
    e1 = jnp.pad(jnp.eye(ROPE_DIM, dtype=F32), ((0, LANES - ROPE_DIM), (NOPE_DIM, LANES - hd)))
    e8 = jnp.tile(e1, (1, MLA_HEADS)).astype(BF16)
    wuv = w_uv.reshape(KV_LORA, MLA_HEADS * V_DIM).astype(BF16)
    wukt = jnp.pad(jnp.transpose(w_uk, (1, 2, 0)), ((0, 0), (0, LANES - NOPE_DIM), (0, 0))).astype(BF16)
    wuv3 = jnp.transpose(w_uv, (1, 0, 2)).astype(BF16)
    return w1, w2, wuq, wukp, e8, wuv, wukt, wuv3, w_ffn_in.astype(BF16)


def kernel(x_prompt, x_sample, cache_kv_latent, cache_k_rope, state_retention, state_ffn_conv, page_table, meta_tokens, ln0_g, ln0_b, w_in, q_norm_g, kv_norm_g, w_uq, w_uk, w_uv, ret_gn_g, w_attn_out, w_ret_out, w_o, ln1_g, ln1_b, w_ffn_in, conv_w, conv_b, w_ffn_out, ln2_g, ln2_b):
    assert w_in.shape[0] == DEPTH == 1
    nb, seq, _ = x_prompt.shape
    db, dec_seq, _ = x_sample.shape
    n_pages = page_table.shape[1]
    assert seq % TILE == 0 and dec_seq == 1 and db % RS_ROWS == 0 and n_pages % (2 * PAGES_PER_CHUNK) == 0
    nt = seq // TILE + 1
    plen = nt * TILE
    past = n_pages * PAGE_SIZE

    w1, w2, wuq, wukp, e8, wuv, wukt, wuv3, wfi = _prep_weights(w_in[0], w_uq[0], w_uk[0], w_uv[0], w_ffn_in[0])
    wao = w_attn_out[0].astype(BF16)
    wro = w_ret_out[0].astype(BF16)
    wo = w_o[0].astype(BF16)
    wfo = w_ffn_out[0].astype(BF16)
    row = lambda v: v.reshape(1, -1)
    g0, b0 = row(ln0_g), row(ln0_b)
    g1, b1, g2, b2 = row(ln1_g[0]), row(ln1_b[0]), row(ln2_g[0]), row(ln2_b[0])
    qg, kg, gn = row(q_norm_g[0]), row(kv_norm_g[0]), row(ret_gn_g[0])
    cw, cb = conv_w[0], row(conv_b[0])

    meta_tile = jnp.concatenate([jnp.zeros((FRONT_PAD, D_MODEL), F32), meta_tokens.astype(F32)], 0)
    tab_p = _rope_tables(jnp.maximum(jnp.arange(plen, dtype=F32) - FRONT_PAD, 0.0))
    tab_s = jnp.broadcast_to(_rope_tables(jnp.full((1,), past, F32)), (db, N_TAB * LANES))

    tok = lambda w: pl.BlockSpec((1, TILE, w), lambda b, t: (b, t, 0))
    pair_tok = lambda w: pl.BlockSpec((1, N_PAIRS, TILE, w), lambda b, t: (b, 0, t, 0))
    pair_seq = lambda w: pl.BlockSpec((1, N_PAIRS, plen, w), lambda b, t: (b, 0, 0, 0))
    x_spec = pl.BlockSpec((1, TILE, D_MODEL), lambda b, t: (b, jnp.maximum(t - 1, 0), 0))
    wspec = _whole()
    sds = jax.ShapeDtypeStruct

    q_p, k_p, v_p, c_p, kr_p, rq_p, rk_p, rv_p = pl.pallas_call(
        _inproj_prompt_kernel,
        grid=(nb, nt),
        in_specs=[x_spec, wspec, pl.BlockSpec((TILE, N_TAB * LANES), lambda b, t: (t, 0))] + [wspec] * 9,
        out_specs=[pair_tok(2 * LANES), pair_tok(2 * LANES), pair_tok(2 * V_DIM), tok(KV_LORA), tok(ROPE_DIM),
                   tok(RET_W), tok(RET_W), tok(RET_W)],
        out_shape=[sds((nb, N_PAIRS, plen, 2 * LANES), BF16), sds((nb, N_PAIRS, plen, 2 * LANES), BF16),
                   sds((nb, N_PAIRS, plen, 2 * V_DIM), BF16), sds((nb, plen, KV_LORA), F32),
                   sds((nb, plen, ROPE_DIM), F32), sds((nb, plen, RET_W), F32),
                   sds((nb, plen, RET_W), F32), sds((nb, plen, RET_W), F32)],
        compiler_params=_params(2),
        name="inproj_prompt",
    )(x_prompt, meta_tile, tab_p, g0, b0, w1, qg, kg, wuq, wukp, e8, wuv)

    bounds = tuple(sorted({-(-nt * c // ATTN_CLASSES) for c in range(1, ATTN_CLASSES + 1)}))
    ao_p = pl.pallas_call(
        functools.partial(_attn_prompt_kernel, bounds=bounds),
        grid=(nb, nt),
        in_specs=[pair_tok(2 * LANES), pair_seq(2 * LANES), pair_seq(2 * V_DIM)],
        out_specs=pair_tok(2 * V_DIM),
        out_shape=sds((nb, N_PAIRS, plen, 2 * V_DIM), BF16),
        compiler_params=_params(2),
        name="attn_prompt",
    )(q_p, k_p, v_p)

    ro_p, st_p = pl.pallas_call(
        _ret_prompt_kernel,
        grid=(nb, nt),
        in_specs=[tok(RET_W), tok(RET_W), tok(RET_W), wspec],
        out_specs=[tok(RET_W), pl.BlockSpec((1, RET_HEADS, RET_DK, RET_DV), lambda b, t: (b, 0, 0, 0))],
        out_shape=[sds((nb, plen, RET_W), F32), sds((nb, RET_HEADS, RET_DK, RET_DV), F32)],
        scratch_shapes=[pltpu.VMEM((RET_HEADS, RET_DK, RET_DV), F32)],
        compiler_params=_params(2),
        name="ret_prompt",
    )(rq_p, rk_p, rv_p, gn)

    y_p, cs_p = pl.pallas_call(
        _merge_ffn_prompt_kernel,
        grid=(nb, nt),
        in_specs=[x_spec, wspec, pair_tok(2 * V_DIM), tok(RET_W)] + [wspec] * 14,
        out_specs=[pl.BlockSpec((1, TILE, D_MODEL), lambda b, t: (b, jnp.maximum(t - 1, 0), 0)),
                   pl.BlockSpec((1, CONV_W - 1, D_FF), lambda b, t: (b, 0, 0))],
        out_shape=[sds((nb, seq, D_MODEL), F32), sds((nb, CONV_W - 1, D_FF), F32)],
        scratch_shapes=[pltpu.VMEM((8, D_FF), F32)],
        compiler_params=_params(2),
        name="merge_ffn_prompt",
    )(x_prompt, meta_tile, ao_p, ro_p, g0, b0, w2, wao, wro, wo, g1, b1, wfi, cw, cb, wfo, g2, b2)

    xs = x_sample.reshape(db, D_MODEL)
    q_s, qabs_s, c_s, kr_s, rq_s, rk_s, rv_s = pl.pallas_call(
        _inproj_sample_kernel,
        in_specs=[wspec] * 9,
        out_specs=[wspec] * 7,
        out_shape=[sds((db, HEAD_W), BF16), sds((MLA_HEADS, db, KV_LORA), BF16), sds((db, KV_LORA), F32),
                   sds((db, ROPE_DIM), F32), sds((db, RET_W), F32), sds((db, RET_W), F32), sds((db, RET_W), F32)],
        compiler_params=pltpu.CompilerParams(vmem_limit_bytes=VMEM_LIMIT),
        name="inproj_sample",
    )(xs, tab_s, g0, b0, w1, qg, kg, wuq, wukt)

    qabs_t = jnp.transpose(qabs_s, (1, 0, 2))
    qr_s = q_s.reshape(db, MLA_HEADS, LANES)[:, :, NOPE_DIM:NOPE_DIM + ROPE_DIM]
    n_chunks = n_pages // PAGES_PER_CHUNK
    chunk_rows = PAGES_PER_CHUNK * PAGE_SIZE
    per_req = lambda s1, s2: pl.BlockSpec((1, s1, s2), lambda b, pt: (b, 0, 0))
    olat_s = pl.pallas_call(
        functools.partial(_attn_sample_kernel, n_chunks=n_chunks),
        grid_spec=pltpu.PrefetchScalarGridSpec(
            num_scalar_prefetch=1,
            grid=(db,),
            in_specs=[per_req(MLA_HEADS, KV_LORA), per_req(MLA_HEADS, ROPE_DIM), per_req(1, KV_LORA),
                      per_req(1, ROPE_DIM), _whole(pl.ANY), _whole(pl.ANY)],
            out_specs=per_req(MLA_HEADS, KV_LORA),
            scratch_shapes=[pltpu.VMEM((2, chunk_rows, KV_LORA), F32),
                            pltpu.VMEM((2, chunk_rows, ROPE_DIM), F32),
                            pltpu.VMEM((2, chunk_rows, KV_LORA), BF16),
                            pltpu.SemaphoreType.DMA((2, 2))]),
        out_shape=sds((db, MLA_HEADS, KV_LORA), F32),
        compiler_params=_params(1),
        name="attn_sample",
    )(page_table, qabs_t, qr_s, c_s.reshape(db, 1, KV_LORA), kr_s.reshape(db, 1, ROPE_DIM),
      cache_kv_latent, cache_k_rope)

    rows = lambda w: pl.BlockSpec((RS_ROWS, w), lambda i: (i, 0))
    st_spec = pl.BlockSpec((RS_ROWS, RET_HEADS, RET_DK, RET_DV), lambda i: (i, 0, 0, 0))
    ro_s, st_s = pl.pallas_call(
        _ret_sample_kernel,
        grid=(db // RS_ROWS,),
        in_specs=[rows(RET_W), rows(RET_W), rows(RET_W), st_spec, wspec],
        out_specs=[rows(RET_W), st_spec],
        out_shape=[sds((db, RET_W), F32), sds((db, RET_HEADS, RET_DK, RET_DV), F32)],
        compiler_params=_params(1),
        name="ret_sample",
    )(rq_s, rk_s, rv_s, state_retention.reshape(db, RET_HEADS, RET_DK, RET_DV), gn)

    y_s, cs_s = pl.pallas_call(
        _merge_ffn_sample_kernel,
        in_specs=[wspec] * 19,
        out_specs=[wspec] * 2,
        out_shape=[sds((db, D_MODEL), F32), sds((db, (CONV_W - 1) * D_FF), F32)],
        compiler_params=pltpu.CompilerParams(vmem_limit_bytes=VMEM_LIMIT),
        name="merge_ffn_sample",
    )(xs, jnp.transpose(olat_s, (1, 0, 2)), ro_s, state_ffn_conv.reshape(db, (CONV_W - 1) * D_FF),
      g0, b0, w2, wuv3, wao, wro, wo, g1, b1, wfi, cw, cb, wfo, g2, b2)

    first = FRONT_PAD
    return (y_p,
            y_s.reshape(db, 1, D_MODEL),
            c_p[None, :, first:],
            kr_p[None, :, first:],
            st_p[None],
            cs_p[None],
            c_s.reshape(1, db, 1, KV_LORA),
            kr_s.reshape(1, db, 1, ROPE_DIM),
            st_s[None],
            cs_s.reshape(1, db, CONV_W - 1, D_FF))
```

```python
import functools

import numpy as np
import jax
import jax.numpy as jnp
from jax import lax
from jax.experimental import pallas as pl
from jax.experimental.pallas import tpu as pltpu

F32 = jnp.float32
BF16 = jnp.bfloat16

D_MODEL = 1024
N_META = 16
MLA_HEADS = 8
Q_LORA = 384
KV_LORA = 256
NOPE_DIM = 64
ROPE_DIM = 32
V_DIM = 64
ROPE_THETA = 10000.0
RET_HEADS = 4
RET_DK = 128
RET_DV = 128
D_FF = 2816
CONV_W = 3
PAGE_SIZE = 128
LN_EPS = 1e-5
RMS_EPS = 1e-6
DEPTH = 1
ALPHA = (2 * DEPTH) ** 0.25

LANES = 128
TILE = 128
FRONT_PAD = TILE - N_META
RET_W = RET_HEADS * RET_DK
HEAD_W = MLA_HEADS * LANES
N_PAIRS = MLA_HEADS // 2
ATTN_CLASSES = 4
N_TAB = 7
W1_COLS = Q_LORA + KV_LORA + 3 * RET_W + 2 * LANES
W2_COLS = RET_W + 2 * D_MODEL
NEG = -1e30
PAGES_PER_CHUNK = 32
IN_TM = 512
MF_TM = 256
VMEM_LIMIT = 56 * 1024 * 1024


def _dot(a, b):
    return jnp.dot(a, b, preferred_element_type=F32)


def _dot_nt(a, b):
    return lax.dot_general(a, b, (((1,), (1,)), ((), ())), preferred_element_type=F32)


def _layer_norm(x, g, b):
    mu = jnp.mean(x, -1, keepdims=True)
    xc = x - mu
    var = jnp.mean(xc * xc, -1, keepdims=True)
    return xc * lax.rsqrt(var + LN_EPS) * g + b


def _rms_norm(x, g):
    return x * lax.rsqrt(jnp.mean(x * x, -1, keepdims=True) + RMS_EPS) * g


def _group_norm(o):
    mu = jnp.mean(o, -1, keepdims=True)
    oc = o - mu
    var = jnp.mean(oc * oc, -1, keepdims=True)
    return oc * lax.rsqrt(var + LN_EPS)


def _log_gamma(h):
    return float(np.log(np.float32(1.0) - np.float32(2.0) ** np.float32(-5.0 - h), dtype=np.float32))


def _exp32(v):
    return float(np.exp(np.float32(v), dtype=np.float32))


def _whole(memory_space=pltpu.VMEM):
    return pl.BlockSpec(memory_space=memory_space)


def _params(n_axes):
    return pltpu.CompilerParams(dimension_semantics=("arbitrary",) * n_axes, vmem_limit_bytes=VMEM_LIMIT)


def _inproj_common(x, tab, g0, b0, w1, qg, kg, wuq):
    xn = _layer_norm(x, g0, b0)
    h = _dot(xn.astype(BF16), w1)
    o = 0
    q_lat = h[:, o:o + Q_LORA]; o += Q_LORA
    c_raw = h[:, o:o + KV_LORA]; o += KV_LORA
    rq = h[:, o:o + RET_W]; o += RET_W
    rk = h[:, o:o + RET_W]; o += RET_W
    rv = h[:, o:o + RET_W]; o += RET_W
    kr = h[:, o:o + LANES]; o += LANES
    krs = h[:, o:o + LANES]
    qc, qs1, qs2, rc, rs, kc, ks = [tab[:, i * LANES:(i + 1) * LANES] for i in range(N_TAB)]
    c = _rms_norm(c_raw, kg)
    krope = kr * kc + krs * ks
    q = _dot(_rms_norm(q_lat, qg).astype(BF16), wuq)
    q_heads = []
    for hh in range(MLA_HEADS):
        qh = q[:, hh * LANES:(hh + 1) * LANES]
        q_heads.append(qh * qc + pltpu.roll(qh, ROPE_DIM // 2, 1) * qs1
                       + pltpu.roll(qh, LANES - ROPE_DIM // 2, 1) * qs2)

    def rope_ret(x):
        return [x[:, hh * RET_DK:(hh + 1) * RET_DK] * rc
                + pltpu.roll(x[:, hh * RET_DK:(hh + 1) * RET_DK], RET_DK // 2, 1) * rs
                for hh in range(RET_HEADS)]

    rq_h = rope_ret(rq)
    rk_h = [v * (RET_DK ** -0.5) for v in rope_ret(rk)]
    return q_heads, c, krope, rq_h, rk_h, rv


def _position_rows(tab_ref, tm, plen):
    row = lax.broadcasted_iota(jnp.int32, (TILE, 1), 0)
    tabs, pos = [], []
    for j in range(tm // TILE):
        p0 = pl.multiple_of((pl.program_id(0) * tm + j * TILE) % plen, TILE)
        if tab_ref is not None:
            tabs.append(tab_ref[pl.ds(p0, TILE), :])
        pos.append(p0 + row)
    return (jnp.concatenate(tabs, 0) if tabs else None), jnp.concatenate(pos, 0)


def _inproj_prompt_kernel(x_ref, tab_ref, g0, b0, w1, qg, kg, wuq, wukp, e8, wuv,
                          q_o, k_o, v_o, c_o, kr_o, rq_o, rk_o, rv_o, *, plen):
    tab, pos = _position_rows(tab_ref, x_ref.shape[0], plen)
    real = pos >= FRONT_PAD
    q_heads, c, krope, rq_h, rk_h, rv = _inproj_common(
        x_ref[...], tab, g0[...], b0[...], w1[...], qg[...], kg[...], wuq[...])
    for hh in range(MLA_HEADS):
        q_o[hh // 2, :, (hh % 2) * LANES:(hh % 2 + 1) * LANES] = q_heads[hh].astype(BF16)
    for hh in range(RET_HEADS):
        sl = slice(hh * RET_DK, (hh + 1) * RET_DK)
        rq_o[:, sl] = rq_h[hh]
        rk_o[:, sl] = jnp.where(real, rk_h[hh], 0.0)
    rv_o[...] = rv
    cb = c.astype(BF16)
    k = (_dot(cb, wukp[...]) + _dot(krope.astype(BF16), e8[...])).astype(BF16)
    v = _dot(cb, wuv[...]).astype(BF16)
    for pr in range(N_PAIRS):
        k_o[pr] = k[:, pr * 2 * LANES:(pr + 1) * 2 * LANES]
        v_o[pr] = v[:, pr * 2 * V_DIM:(pr + 1) * 2 * V_DIM]
    c_o[...] = c
    kr_o[...] = krope[:, :ROPE_DIM]


def _inproj_sample_kernel(x_ref, tab_ref, g0, b0, w1, qg, kg, wuq, wukt,
                          q_o, qabs_o, c_o, kr_o, rq_o, rk_o, rv_o):
    q_heads, c, krope, rq_h, rk_h, rv = _inproj_common(
        x_ref[...], tab_ref[...], g0[...], b0[...], w1[...], qg[...], kg[...], wuq[...])
    for hh in range(MLA_HEADS):
        qb = q_heads[hh].astype(BF16)
        q_o[:, hh * LANES:(hh + 1) * LANES] = qb
        qabs_o[hh] = _dot(qb, wukt[hh]).astype(BF16)
    for hh in range(RET_HEADS):
        sl = slice(hh * RET_DK, (hh + 1) * RET_DK)
        rq_o[:, sl] = rq_h[hh]
        rk_o[:, sl] = rk_h[hh]
    rv_o[...] = rv
    c_o[...] = c
    kr_o[...] = krope[:, :ROPE_DIM]


def _attn_prompt_kernel(q_ref, k_ref, v_ref, o_ref, *, bounds):
    qi = pl.program_id(1)
    row = lax.broadcasted_iota(jnp.int32, (TILE, TILE), 0) + qi * TILE
    lane = lax.broadcasted_iota(jnp.int32, (TILE, TILE), 1)
    first = lane < V_DIM

    def run(lo, wt):
        width = wt * TILE

        def pair_body(pr, carry):
            q = q_ref[pr]
            k = k_ref[pr, :width, :]
            v = v_ref[pr, :width, :]
            outs = []
            for hh in range(2):
                s = _dot_nt(q[:, hh * LANES:(hh + 1) * LANES], k[:, hh * LANES:(hh + 1) * LANES])
                tiles = []
                for c in range(wt):
                    sc = s[:, c * TILE:(c + 1) * TILE]
                    if c == 0:
                        sc = jnp.where(lane >= FRONT_PAD, sc, NEG)
                    if c >= lo:
                        sc = jnp.where(lane + c * TILE <= row, sc, NEG)
                    tiles.append(sc)
                s = jnp.concatenate(tiles, 1)
                p = jnp.exp(s - jnp.max(s, -1, keepdims=True))
                inv_l = 1.0 / jnp.sum(p, -1, keepdims=True)
                outs.append(_dot(p.astype(BF16), v) * inv_l)
            o_ref[pr] = jnp.where(first, outs[0], outs[1]).astype(BF16)
            return carry

        lax.fori_loop(0, N_PAIRS, pair_body, 0, unroll=True)

    lo = 0
    for wt in bounds:
        pl.when((qi >= lo) & (qi < wt))(functools.partial(run, lo, wt))
        lo = wt


def _ret_prompt_kernel(rq_ref, rk_ref, rv_ref, gn_ref, o_ref, st_ref, state):
    t = pl.program_id(1)

    @pl.when(t == 0)
    def _():
        state[...] = jnp.zeros_like(state)

    ii = lax.broadcasted_iota(jnp.int32, (TILE, TILE), 0).astype(F32)
    jj = lax.broadcasted_iota(jnp.int32, (TILE, TILE), 1).astype(F32)
    diff = ii - jj
    idx = lax.broadcasted_iota(jnp.int32, (TILE, 1), 0).astype(F32)
    for hh in range(RET_HEADS):
        sl = slice(hh * RET_DK, (hh + 1) * RET_DK)
        lg = _log_gamma(hh)
        decay = jnp.where(diff >= 0, jnp.exp(jnp.maximum(diff, 0.0) * lg), 0.0)
        q = rq_ref[0, :, sl]
        k = rk_ref[0, :, sl]
        vb = rv_ref[0, :, sl].astype(BF16)
        st = state[hh]
        inner = _dot_nt(q.astype(BF16), k.astype(BF16)) * decay
        o = _dot(inner.astype(BF16), vb) + _dot((q * jnp.exp((idx + 1.0) * lg)).astype(BF16), st.astype(BF16))
        k_dec = k * jnp.exp((TILE - 1.0 - idx) * lg)
        state[hh] = _exp32(TILE * lg) * st + _dot(k_dec.T.astype(BF16), vb)
        o_ref[0, :, sl] = _group_norm(o) * gn_ref[:, sl]

    @pl.when(t == pl.num_programs(1) - 1)
    def _():
        st_ref[0] = state[...]


def _merge(xn, attn_proj, ron, w2, wro, wo, g1, b1):
    g3 = _dot(xn.astype(BF16), w2)
    rg = g3[:, :RET_W]
    ga = g3[:, RET_W:RET_W + D_MODEL]
    gb = g3[:, RET_W + D_MODEL:]
    ret = (ron * (rg * jax.nn.sigmoid(rg))).astype(BF16)
    m = jax.nn.sigmoid(ga) * attn_proj + jax.nn.sigmoid(gb) * _dot(ret, wro)
    return _layer_norm(ALPHA * xn + _dot(m.astype(BF16), wo), g1, b1)


def _ffn_tail(x1, a, a1, a2, g, cw, cb, wfo, g2, b2):
    acc = cb + cw[0:1] * a2 + cw[1:2] * a1 + cw[2:3] * a
    hid = (jax.nn.gelu(acc) * g).astype(BF16)
    return _layer_norm(ALPHA * x1 + _dot(hid, wfo), g2, b2)


def _merge_ffn_prompt_kernel(x_ref, ao_ref, ro_ref, g0, b0, w2, wao, wro, wo, g1, b1,
                             wfi, cw, cb, wfo, g2, b2, y_ref, cs_ref, carry, *, nb, plen):
    i = pl.program_id(0)
    tm = x_ref.shape[0]
    xn = _layer_norm(x_ref[...], g0[...], b0[...])
    attn_proj = None
    for pr in range(N_PAIRS):
        part = _dot(ao_ref[pr], wao[pr * 2 * V_DIM:(pr + 1) * 2 * V_DIM, :])
        attn_proj = part if attn_proj is None else attn_proj + part
    x1 = _merge(xn, attn_proj, ro_ref[...], w2[...], wro[...], wo[...], g1[...], b1[...])
    x1b = x1.astype(BF16)
    a = _dot(x1b, wfi[:, :D_FF])
    g = _dot(x1b, wfi[:, D_FF:])

    @pl.when(i == 0)
    def _():
        carry[...] = jnp.zeros_like(carry)

    prev = carry[...]
    row = lax.broadcasted_iota(jnp.int32, (tm, 1), 0)
    _, pos = _position_rows(None, tm, plen)
    a1 = jnp.where(row == 0, prev[7:8], pltpu.roll(a, 1, 0))
    a2 = jnp.where(row == 0, prev[6:7], jnp.where(row == 1, prev[7:8], pltpu.roll(a, 2, 0)))
    a1 = jnp.where(pos >= FRONT_PAD + 1, a1, 0.0)
    a2 = jnp.where(pos >= FRONT_PAD + 2, a2, 0.0)
    carry[...] = a[tm - 8:]
    y_ref[...] = _ffn_tail(x1, a, a1, a2, g, cw[...], cb[...], wfo[...], g2[...], b2[...])

    for b in range(nb):
        end = (b + 1) * plen
        local = (end - 1) % tm + 1

        @pl.when(i == (end - 1) // tm)
        def _():
            cs_ref[b] = a[local - (CONV_W - 1):local]


def _merge_ffn_sample_kernel(x_ref, olat_ref, ro_ref, prev_ref, g0, b0, w2, wuv3, wao, wro, wo, g1, b1,
                             wfi, cw, cb, wfo, g2, b2, y_ref, cs_ref):
    xn = _layer_norm(x_ref[...], g0[...], b0[...])
    attn_proj = None
    for hh in range(MLA_HEADS):
        oh = _dot(olat_ref[hh].astype(BF16), wuv3[hh]).astype(BF16)
        part = _dot(oh, wao[hh * V_DIM:(hh + 1) * V_DIM, :])
        attn_proj = part if attn_proj is None else attn_proj + part
    x1 = _merge(xn, attn_proj, ro_ref[...], w2[...], wro[...], wo[...], g1[...], b1[...])
    x1b = x1.astype(BF16)
    a = _dot(x1b, wfi[:, :D_FF])
    g = _dot(x1b, wfi[:, D_FF:])
    prev0 = prev_ref[:, :D_FF]
    prev1 = prev_ref[:, D_FF:]
    y_ref[...] = _ffn_tail(x1, a, prev1, prev0, g, cw[...], cb[...], wfo[...], g2[...], b2[...])
    cs_ref[:, :D_FF] = prev1
    cs_ref[:, D_FF:] = a


def _attn_sample_kernel(pt_ref, qabs_ref, qr_ref, cnew_ref, krnew_ref, cache_c, cache_kr, o_ref,
                        cbuf, krbuf, cb16, sems, *, n_chunks):
    b = pl.program_id(0)
    nb = pl.num_programs(0)

    def copies(bb, ci, slot):
        out = []
        for p in range(PAGES_PER_CHUNK):
            page = pt_ref[bb, ci * PAGES_PER_CHUNK + p]
            rows = pl.ds(p * PAGE_SIZE, PAGE_SIZE)
            out.append((pltpu.make_async_copy(cache_c.at[0, page], cbuf.at[slot, rows], sems.at[0, slot]), p % 2))
            out.append((pltpu.make_async_copy(cache_kr.at[0, page], krbuf.at[slot, :, rows], sems.at[1, slot]),
                        (p + 1) % 2))
        return out

    def start(bb, ci, slot):
        for cp, prio in copies(bb, ci, slot):
            cp.start(priority=prio)

    @pl.when(b == 0)
    def _():
        start(0, 0, 0)

    qabs = qabs_ref[0]
    qr = qr_ref[0]
    m = jnp.full((MLA_HEADS, 1), NEG, F32)
    l = jnp.zeros((MLA_HEADS, 1), F32)
    acc = jnp.zeros((MLA_HEADS, KV_LORA), F32)
    pending = None
    for ci in range(n_chunks):
        slot = ci % 2
        if ci + 1 < n_chunks:
            start(b, ci + 1, 1 - slot)
        else:
            pl.when(b + 1 < nb)(functools.partial(start, b + 1, 0, 1 - slot))
        for cp, _ in copies(b, ci, slot):
            cp.wait()
        cb16[slot] = cbuf[slot].astype(BF16)
        s = _dot_nt(qabs, cb16[slot]) + _dot(qr, krbuf[slot].astype(BF16))
        if pending is not None:
            acc = pending[1] * acc + _dot(pending[0], cb16[pending[2]])
        m_new = jnp.maximum(m, jnp.max(s, -1, keepdims=True))
        p = jnp.exp(s - m_new)
        alpha = jnp.exp(m - m_new)
        l = alpha * l + jnp.sum(p, -1, keepdims=True)
        m = m_new
        pending = (p.astype(BF16), alpha, slot)
    acc = pending[1] * acc + _dot(pending[0], cb16[pending[2]])
    cnew = cnew_ref[0]
    s_new = (jnp.sum(qabs.astype(F32) * cnew, -1, keepdims=True)
             + jnp.sum(qr.astype(F32) * krnew_ref[0], -1, keepdims=True))
    m_new = jnp.maximum(m, s_new)
    p_new = jnp.exp(s_new - m_new)
    alpha = jnp.exp(m - m_new)
    o_ref[0] = (alpha * acc + p_new * cnew) / (alpha * l + p_new)


RS_ROWS = 8


def _ret_sample_kernel(rq_ref, rk_ref, rv_ref, st_ref, gn_ref, o_ref, ns_ref):
    for hh in range(RET_HEADS):
        sl = slice(hh * RET_DK, (hh + 1) * RET_DK)
        gamma = _exp32(_log_gamma(hh))
        q8 = rq_ref[:, sl]
        k8 = rk_ref[:, sl]
        v8 = rv_ref[:, sl]
        qk = jnp.sum(q8 * k8, -1, keepdims=True)
        for r in range(RS_ROWS):
            kcol = jnp.broadcast_to(k8[r:r + 1], (RET_DK, RET_DK)).T
            qcol = jnp.broadcast_to(q8[r:r + 1], (RET_DK, RET_DK)).T
            st = st_ref[r, hh]
            v = v8[r:r + 1]
            ns_ref[r, hh] = gamma * st + kcol * v
            o = qk[r:r + 1] * v + jnp.sum(qcol * gamma * st, 0, keepdims=True)
            o_ref[r:r + 1, sl] = _group_norm(o) * gn_ref[:, sl]


def _rope_tables(pos):
    n = pos.shape[0]

    def cos_sin(half):
        inv = ROPE_THETA ** (-jnp.arange(half, dtype=F32) / half)
        ang = pos[:, None] * inv[None, :]
        return jnp.cos(ang), jnp.sin(ang)

    c16, s16 = cos_sin(ROPE_DIM // 2)
    c64, s64 = cos_sin(RET_DK // 2)
    z = lambda w: jnp.zeros((n, w), F32)
    scale = (NOPE_DIM + ROPE_DIM) ** -0.5
    tail = LANES - NOPE_DIM - ROPE_DIM
    qc = jnp.concatenate([jnp.ones((n, NOPE_DIM), F32), c16, c16, z(tail)], 1) * scale
    qs1 = jnp.concatenate([z(NOPE_DIM + ROPE_DIM // 2), s16, z(tail)], 1) * scale
    qs2 = jnp.concatenate([z(NOPE_DIM), -s16, z(ROPE_DIM // 2 + tail)], 1) * scale
    rc = jnp.concatenate([c64, c64], 1)
    rs = jnp.concatenate([-s64, s64], 1)
    kc = jnp.concatenate([c16, c16, z(LANES - ROPE_DIM)], 1)
    ks = jnp.concatenate([-s16, s16, z(LANES - ROPE_DIM)], 1)
    return jnp.concatenate([qc, qs1, qs2, rc, rs, kc, ks], 1)


def _prep_weights(w_in, w_uq, w_uk, w_uv, w_ffn_in):
    sizes = (Q_LORA, KV_LORA, ROPE_DIM, RET_W, RET_W, RET_W, RET_W, D_MODEL, D_MODEL)
    offs = [0]
    for s in sizes:
        offs.append(offs[-1] + s)
    col = lambda i: w_in[:, offs[i]:offs[i + 1]]
    w_kr = col(2)
    half = ROPE_DIM // 2
    zpad = jnp.zeros((D_MODEL, LANES - ROPE_DIM), F32)
    w1 = jnp.concatenate([col(0), col(1), col(3), col(4), col(5), w_kr, zpad,
                          w_kr[:, half:], w_kr[:, :half], zpad], 1).astype(BF16)
    w2 = jnp.concatenate([col(6), col(7), col(8)], 1).astype(BF16)
    hd = NOPE_DIM + ROPE_DIM
    wuq = jnp.pad(w_uq, ((0, 0), (0, 0), (0, LANES - hd))).reshape(Q_LORA, HEAD_W).astype(BF16)
    wukp = jnp.pad(w_uk, ((0, 0), (0, 0), (0, LANES - NOPE_DIM))).reshape(KV_LORA, HEAD_W).astype(BF16)
    e1 = jnp.pad(jnp.eye(ROPE_DIM, dtype=F32), ((0, LANES - ROPE_DIM), (NOPE_DIM, LANES - hd)))
    e8 = jnp.tile(e1, (1, MLA_HEADS)).astype(BF16)
    wuv = w_uv.reshape(KV_LORA, MLA_HEADS * V_DIM).astype(BF16)
    wukt = jnp.pad(jnp.transpose(w_uk, (1, 2, 0)), ((0, 0), (0, LANES - NOPE_DIM), (0, 0))).astype(BF16)
    wuv3 = jnp.transpose(w_uv, (1, 0, 2)).astype(BF16)
    return w1, w2, wuq, wukp, e8, wuv, wukt, wuv3, w_ffn_in.astype(BF16)


def kernel(x_prompt, x_sample, cache_kv_latent, cache_k_rope, state_retention, state_ffn_conv, page_table, meta_tokens, ln0_g, ln0_b, w_in, q_norm_g, kv_norm_g, w_uq, w_uk, w_uv, ret_gn_g, w_attn_out, w_ret_out, w_o, ln1_g, ln1_b, w_ffn_in, conv_w, conv_b, w_ffn_out, ln2_g, ln2_b):
    assert w_in.shape[0] == DEPTH == 1
    nb, seq, _ = x_prompt.shape
    db, dec_seq, _ = x_sample.shape
    n_pages = page_table.shape[1]
    assert seq % TILE == 0 and dec_seq == 1 and db % RS_ROWS == 0 and n_pages % (2 * PAGES_PER_CHUNK) == 0
    nt = seq // TILE + 1
    plen = nt * TILE
    past = n_pages * PAGE_SIZE

    w1, w2, wuq, wukp, e8, wuv, wukt, wuv3, wfi = _prep_weights(w_in[0], w_uq[0], w_uk[0], w_uv[0], w_ffn_in[0])
    wao = w_attn_out[0].astype(BF16)
    wro = w_ret_out[0].astype(BF16)
    wo = w_o[0].astype(BF16)
    wfo = w_ffn_out[0].astype(BF16)
    row = lambda v: v.reshape(1, -1)
    g0, b0 = row(ln0_g), row(ln0_b)
    g1, b1, g2, b2 = row(ln1_g[0]), row(ln1_b[0]), row(ln2_g[0]), row(ln2_b[0])
    qg, kg, gn = row(q_norm_g[0]), row(kv_norm_g[0]), row(ret_gn_g[0])
    cw, cb = conv_w[0], row(conv_b[0])

    tab_p = _rope_tables(jnp.maximum(jnp.arange(plen, dtype=F32) - FRONT_PAD, 0.0))
    tab_s = jnp.broadcast_to(_rope_tables(jnp.full((1,), past, F32)), (db, N_TAB * LANES))
    n_rows = nb * plen
    xp = jnp.concatenate([jnp.zeros((nb, FRONT_PAD, D_MODEL), F32),
                          jnp.broadcast_to(meta_tokens.astype(F32)[None], (nb, N_META, D_MODEL)),
                          x_prompt], 1).reshape(n_rows, D_MODEL)
    in_tm = IN_TM if n_rows % IN_TM == 0 else TILE
    mf_tm = MF_TM if n_rows % MF_TM == 0 else TILE

    flat = lambda tm, w: pl.BlockSpec((tm, w), lambda i: (i, 0))
    pair_flat = lambda tm, w: pl.BlockSpec((N_PAIRS, tm, w), lambda i: (0, i, 0))
    tok = lambda w: pl.BlockSpec((1, TILE, w), lambda b, t: (b, t, 0))
    pair_tok = lambda w: pl.BlockSpec((N_PAIRS, TILE, w), lambda b, t: (0, b * nt + t, 0))
    pair_seq = lambda w: pl.BlockSpec((N_PAIRS, plen, w), lambda b, t: (0, b, 0))
    wspec = _whole()
    sds = jax.ShapeDtypeStruct

    q_p, k_p, v_p, c_p, kr_p, rq_p, rk_p, rv_p = pl.pallas_call(
        functools.partial(_inproj_prompt_kernel, plen=plen),
        grid=(n_rows // in_tm,),
        in_specs=[flat(in_tm, D_MODEL)] + [wspec] * 10,
        out_specs=[pair_flat(in_tm, 2 * LANES), pair_flat(in_tm, 2 * LANES), pair_flat(in_tm, 2 * V_DIM),
                   flat(in_tm, KV_LORA), flat(in_tm, ROPE_DIM), flat(in_tm, RET_W), flat(in_tm, RET_W),
                   flat(in_tm, RET_W)],
        out_shape=[sds((N_PAIRS, n_rows, 2 * LANES), BF16), sds((N_PAIRS, n_rows, 2 * LANES), BF16),
                   sds((N_PAIRS, n_rows, 2 * V_DIM), BF16), sds((n_rows, KV_LORA), F32),
                   sds((n_rows, ROPE_DIM), F32), sds((n_rows, RET_W), F32),
                   sds((n_rows, RET_W), F32), sds((n_rows, RET_W), F32)],
        compiler_params=_params(1),
        name="inproj_prompt",
    )(xp, tab_p, g0, b0, w1, qg, kg, wuq, wukp, e8, wuv)

    bounds = tuple(sorted({-(-nt * c // ATTN_CLASSES) for c in range(1, ATTN_CLASSES + 1)}))
    ao_p = pl.pallas_call(
        functools.partial(_attn_prompt_kernel, bounds=bounds),
        grid=(nb, nt),
        in_specs=[pair_tok(2 * LANES), pair_seq(2 * LANES), pair_seq(2 * V_DIM)],
        out_specs=pair_tok(2 * V_DIM),
        out_shape=sds((N_PAIRS, n_rows, 2 * V_DIM), BF16),
        compiler_params=_params(2),
        name="attn_prompt",
    )(q_p, k_p, v_p)

    seq3 = lambda v: v.reshape(nb, plen, v.shape[-1])
    ro_p, st_p = pl.pallas_call(
        _ret_prompt_kernel,
        grid=(nb, nt),
        in_specs=[tok(RET_W), tok(RET_W), tok(RET_W), wspec],
        out_specs=[tok(RET_W), pl.BlockSpec((1, RET_HEADS, RET_DK, RET_DV), lambda b, t: (b, 0, 0, 0))],
        out_shape=[sds((nb, plen, RET_W), F32), sds((nb, RET_HEADS, RET_DK, RET_DV), F32)],
        scratch_shapes=[pltpu.VMEM((RET_HEADS, RET_DK, RET_DV), F32)],
        compiler_params=_params(2),
        name="ret_prompt",
    )(seq3(rq_p), seq3(rk_p), seq3(rv_p), gn)

    y_pad, cs_p = pl.pallas_call(
        functools.partial(_merge_ffn_prompt_kernel, nb=nb, plen=plen),
        grid=(n_rows // mf_tm,),
        in_specs=[flat(mf_tm, D_MODEL), pair_flat(mf_tm, 2 * V_DIM), flat(mf_tm, RET_W)] + [wspec] * 14,
        out_specs=[flat(mf_tm, D_MODEL), wspec],
        out_shape=[sds((n_rows, D_MODEL), F32), sds((nb, CONV_W - 1, D_FF), F32)],
        scratch_shapes=[pltpu.VMEM((8, D_FF), F32)],
        compiler_params=_params(1),
        name="merge_ffn_prompt",
    )(xp, ao_p, ro_p.reshape(n_rows, RET_W), g0, b0, w2, wao, wro, wo, g1, b1, wfi, cw, cb, wfo, g2, b2)
    y_p = y_pad.reshape(nb, plen, D_MODEL)[:, TILE:]
    c_p = seq3(c_p)
    kr_p = seq3(kr_p)

    xs = x_sample.reshape(db, D_MODEL)
    q_s, qabs_s, c_s, kr_s, rq_s, rk_s, rv_s = pl.pallas_call(
        _inproj_sample_kernel,
        in_specs=[wspec] * 9,
        out_specs=[wspec] * 7,
        out_shape=[sds((db, HEAD_W), BF16), sds((MLA_HEADS, db, KV_LORA), BF16), sds((db, KV_LORA), F32),
                   sds((db, ROPE_DIM), F32), sds((db, RET_W), F32), sds((db, RET_W), F32), sds((db, RET_W), F32)],
        compiler_params=pltpu.CompilerParams(vmem_limit_bytes=VMEM_LIMIT),
        name="inproj_sample",
    )(xs, tab_s, g0, b0, w1, qg, kg, wuq, wukt)

    qabs_t = jnp.transpose(qabs_s, (1, 0, 2))
    qr_s = q_s.reshape(db, MLA_HEADS, LANES)[:, :, NOPE_DIM:NOPE_DIM + ROPE_DIM]
    n_chunks = n_pages // PAGES_PER_CHUNK
    chunk_rows = PAGES_PER_CHUNK * PAGE_SIZE
    per_req = lambda s1, s2: pl.BlockSpec((1, s1, s2), lambda b, pt: (b, 0, 0))
    olat_s = pl.pallas_call(
        functools.partial(_attn_sample_kernel, n_chunks=n_chunks),
        grid_spec=pltpu.PrefetchScalarGridSpec(
            num_scalar_prefetch=1,
            grid=(db,),
            in_specs=[per_req(MLA_HEADS, KV_LORA), per_req(MLA_HEADS, ROPE_DIM), per_req(1, KV_LORA),
                      per_req(1, ROPE_DIM), _whole(pl.ANY), _whole(pl.ANY)],
            out_specs=per_req(MLA_HEADS, KV_LORA),
            scratch_shapes=[pltpu.VMEM((2, chunk_rows, KV_LORA), F32),
                            pltpu.VMEM((2, ROPE_DIM, chunk_rows), F32),
                            pltpu.VMEM((2, chunk_rows, KV_LORA), BF16),
                            pltpu.SemaphoreType.DMA((2, 2))]),
        out_shape=sds((db, MLA_HEADS, KV_LORA), F32),
        compiler_params=_params(1),
        name="attn_sample",
    )(page_table, qabs_t, qr_s, c_s.reshape(db, 1, KV_LORA), kr_s.reshape(db, 1, ROPE_DIM),
      cache_kv_latent, jnp.swapaxes(cache_k_rope, 2, 3))

    rows = lambda w: pl.BlockSpec((RS_ROWS, w), lambda i: (i, 0))
    st_spec = pl.BlockSpec((RS_ROWS, RET_HEADS, RET_DK, RET_DV), lambda i: (i, 0, 0, 0))
    ro_s, st_s = pl.pallas_call(
        _ret_sample_kernel,
        grid=(db // RS_ROWS,),
        in_specs=[rows(RET_W), rows(RET_W), rows(RET_W), st_spec, wspec],
        out_specs=[rows(RET_W), st_spec],
        out_shape=[sds((db, RET_W), F32), sds((db, RET_HEADS, RET_DK, RET_DV), F32)],
        compiler_params=_params(1),
        name="ret_sample",
    )(rq_s, rk_s, rv_s, state_retention.reshape(db, RET_HEADS, RET_DK, RET_DV), gn)

    y_s, cs_s = pl.pallas_call(
        _merge_ffn_sample_kernel,
        in_specs=[wspec] * 19,
        out_specs=[wspec] * 2,
        out_shape=[sds((db, D_MODEL), F32), sds((db, (CONV_W - 1) * D_FF), F32)],
        compiler_params=pltpu.CompilerParams(vmem_limit_bytes=VMEM_LIMIT),
        name="merge_ffn_sample",
    )(xs, jnp.transpose(olat_s, (1, 0, 2)), ro_s, state_ffn_conv.reshape(db, (CONV_W - 1) * D_FF),
      g0, b0, w2, wuv3, wao, wro, wo, g1, b1, wfi, cw, cb, wfo, g2, b2)

    first = FRONT_PAD
    return (y_p,
            y_s.reshape(db, 1, D_MODEL),
            c_p[None, :, first:],
            kr_p[None, :, first:],
            st_p[None],
            cs_p[None],
            c_s.reshape(1, db, 1, KV_LORA),
            kr_s.reshape(1, db, 1, ROPE_DIM),
            st_s[None],
            cs_s.reshape(1, db, CONV_W - 1, D_FF))
```

```python
import functools

import numpy as np
import jax
import jax.numpy as jnp
from jax import lax
from jax.experimental import pallas as pl
from jax.experimental.pallas import tpu as pltpu

F32 = jnp.float32
BF16 = jnp.bfloat16

D_MODEL = 1024
N_META = 16
MLA_HEADS = 8
Q_LORA = 384
KV_LORA = 256
NOPE_DIM = 64
ROPE_DIM = 32
V_DIM = 64
ROPE_THETA = 10000.0
RET_HEADS = 4
RET_DK = 128
RET_DV = 128
D_FF = 2816
CONV_W = 3
PAGE_SIZE = 128
LN_EPS = 1e-5
RMS_EPS = 1e-6
DEPTH = 1
ALPHA = (2 * DEPTH) ** 0.25

LANES = 128
TILE = 128
FRONT_PAD = TILE - N_META
RET_W = RET_HEADS * RET_DK
HEAD_W = MLA_HEADS * LANES
N_PAIRS = MLA_HEADS // 2
ATTN_CLASSES = 4
N_TAB = 7
W1_COLS = Q_LORA + KV_LORA + 3 * RET_W + 2 * LANES
W2_COLS = RET_W + 2 * D_MODEL
NEG = -1e30
PAGES_PER_CHUNK = 32
IN_TM = 512
MF_TM = 512
MF_SUB = 256
RET_SEQS = 2
DMA_SLOTS = 3
VMEM_LIMIT = 56 * 1024 * 1024


def _dot(a, b):
    return jnp.dot(a, b, preferred_element_type=F32)


def _dot_nt(a, b):
    return lax.dot_general(a, b, (((1,), (1,)), ((), ())), preferred_element_type=F32)


def _layer_norm(x, g, b):
    mu = jnp.mean(x, -1, keepdims=True)
    xc = x - mu
    var = jnp.mean(xc * xc, -1, keepdims=True)
    return xc * lax.rsqrt(var + LN_EPS) * g + b


def _rms_norm(x, g):
    return x * lax.rsqrt(jnp.mean(x * x, -1, keepdims=True) + RMS_EPS) * g


def _group_norm(o):
    mu = jnp.mean(o, -1, keepdims=True)
    oc = o - mu
    var = jnp.mean(oc * oc, -1, keepdims=True)
    return oc * lax.rsqrt(var + LN_EPS)


def _log_gamma(h):
    return float(np.log(np.float32(1.0) - np.float32(2.0) ** np.float32(-5.0 - h), dtype=np.float32))


def _exp32(v):
    return float(np.exp(np.float32(v), dtype=np.float32))


def _whole(memory_space=pltpu.VMEM):
    return pl.BlockSpec(memory_space=memory_space)


def _params(n_axes):
    return pltpu.CompilerParams(dimension_semantics=("arbitrary",) * n_axes, vmem_limit_bytes=VMEM_LIMIT)


def _inproj_common(x, tab, g0, b0, w1, qg, kg, wuq):
    xn = _layer_norm(x, g0, b0)
    h = _dot(xn.astype(BF16), w1)
    o = 0
    q_lat = h[:, o:o + Q_LORA]; o += Q_LORA
    c_raw = h[:, o:o + KV_LORA]; o += KV_LORA
    rq = h[:, o:o + RET_W]; o += RET_W
    rk = h[:, o:o + RET_W]; o += RET_W
    rv = h[:, o:o + RET_W]; o += RET_W
    kr = h[:, o:o + LANES]; o += LANES
    krs = h[:, o:o + LANES]
    qc, qs1, qs2, rc, rs, kc, ks = [tab[:, i * LANES:(i + 1) * LANES] for i in range(N_TAB)]
    c = _rms_norm(c_raw, kg)
    krope = kr * kc + krs * ks
    q = _dot(_rms_norm(q_lat, qg).astype(BF16), wuq)
    q_heads = []
    for hh in range(MLA_HEADS):
        qh = q[:, hh * LANES:(hh + 1) * LANES]
        q_heads.append(qh * qc + pltpu.roll(qh, ROPE_DIM // 2, 1) * qs1
                       + pltpu.roll(qh, LANES - ROPE_DIM // 2, 1) * qs2)

    def rope_ret(x):
        return [x[:, hh * RET_DK:(hh + 1) * RET_DK] * rc
                + pltpu.roll(x[:, hh * RET_DK:(hh + 1) * RET_DK], RET_DK // 2, 1) * rs
                for hh in range(RET_HEADS)]

    rq_h = rope_ret(rq)
    rk_h = [v * (RET_DK ** -0.5) for v in rope_ret(rk)]
    return q_heads, c, krope, rq_h, rk_h, rv


def _position_rows(tab_ref, tm, plen):
    row = lax.broadcasted_iota(jnp.int32, (TILE, 1), 0)
    tabs, pos = [], []
    for j in range(tm // TILE):
        p0 = pl.multiple_of((pl.program_id(0) * tm + j * TILE) % plen, TILE)
        if tab_ref is not None:
            tabs.append(tab_ref[pl.ds(p0, TILE), :])
        pos.append(p0 + row)
    return (jnp.concatenate(tabs, 0) if tabs else None), jnp.concatenate(pos, 0)


def _inproj_prompt_kernel(x_ref, tab_ref, g0, b0, w1, qg, kg, wuq, wukp, e8, wuv,
                          q_o, k_o, v_o, c_o, kr_o, rq_o, rk_o, rv_o, *, plen):
    tab, pos = _position_rows(tab_ref, x_ref.shape[0], plen)
    real = pos >= FRONT_PAD
    q_heads, c, krope, rq_h, rk_h, rv = _inproj_common(
        x_ref[...], tab, g0[...], b0[...], w1[...], qg[...], kg[...], wuq[...])
    for hh in range(MLA_HEADS):
        q_o[hh // 2, :, (hh % 2) * LANES:(hh % 2 + 1) * LANES] = q_heads[hh].astype(BF16)
    for hh in range(RET_HEADS):
        sl = slice(hh * RET_DK, (hh + 1) * RET_DK)
        rq_o[:, sl] = rq_h[hh]
        rk_o[:, sl] = jnp.where(real, rk_h[hh], 0.0)
    rv_o[...] = rv
    cb = c.astype(BF16)
    k = (_dot(cb, wukp[...]) + _dot(krope.astype(BF16), e8[...])).astype(BF16)
    v = _dot(cb, wuv[...]).astype(BF16)
    for pr in range(N_PAIRS):
        k_o[pr] = k[:, pr * 2 * LANES:(pr + 1) * 2 * LANES]
        v_o[pr] = v[:, pr * 2 * V_DIM:(pr + 1) * 2 * V_DIM]
    c_o[...] = c
    kr_o[...] = krope[:, :ROPE_DIM]


def _inproj_sample_kernel(x_ref, tab_ref, g0, b0, w1, qg, kg, wuq, wukt,
                          q_o, qabs_o, c_o, kr_o, rq_o, rk_o, rv_o):
    q_heads, c, krope, rq_h, rk_h, rv = _inproj_common(
        x_ref[...], tab_ref[...], g0[...], b0[...], w1[...], qg[...], kg[...], wuq[...])
    for hh in range(MLA_HEADS):
        qb = q_heads[hh].astype(BF16)
        q_o[:, hh * LANES:(hh + 1) * LANES] = qb
        qabs_o[hh] = _dot(qb, wukt[hh]).astype(BF16)
    for hh in range(RET_HEADS):
        sl = slice(hh * RET_DK, (hh + 1) * RET_DK)
        rq_o[:, sl] = rq_h[hh]
        rk_o[:, sl] = rk_h[hh]
    rv_o[...] = rv
    c_o[...] = c
    kr_o[...] = krope[:, :ROPE_DIM]


def _attn_prompt_kernel(q_ref, k_ref, v_ref, o_ref, *, bounds):
    qi = pl.program_id(1)
    row = lax.broadcasted_iota(jnp.int32, (TILE, TILE), 0) + qi * TILE
    lane = lax.broadcasted_iota(jnp.int32, (TILE, TILE), 1)
    first = lane < V_DIM

    def run(lo, wt):
        width = wt * TILE

        def pair_body(pr, carry):
            q = q_ref[pr]
            k = k_ref[pr, :width, :]
            v = v_ref[pr, :width, :]
            outs = []
            for hh in range(2):
                s = _dot_nt(q[:, hh * LANES:(hh + 1) * LANES], k[:, hh * LANES:(hh + 1) * LANES])
                tiles = []
                for c in range(wt):
                    sc = s[:, c * TILE:(c + 1) * TILE]
                    if c == 0:
                        sc = jnp.where(lane >= FRONT_PAD, sc, NEG)
                    if c >= lo:
                        sc = jnp.where(lane + c * TILE <= row, sc, NEG)
                    tiles.append(sc)
                s = jnp.concatenate(tiles, 1)
                p = jnp.exp(s - jnp.max(s, -1, keepdims=True))
                inv_l = 1.0 / jnp.sum(p, -1, keepdims=True)
                outs.append(_dot(p.astype(BF16), v) * inv_l)
            o_ref[pr] = jnp.where(first, outs[0], outs[1]).astype(BF16)
            return carry

        lax.fori_loop(0, N_PAIRS, pair_body, 0, unroll=True)

    lo = 0
    for wt in bounds:
        pl.when((qi >= lo) & (qi < wt))(functools.partial(run, lo, wt))
        lo = wt


def _ret_prompt_kernel(rq_ref, rk_ref, rv_ref, gn_ref, o_ref, st_ref, state):
    t = pl.program_id(1)

    @pl.when(t == 0)
    def _():
        state[...] = jnp.zeros_like(state)

    ii = lax.broadcasted_iota(jnp.int32, (TILE, TILE), 0).astype(F32)
    jj = lax.broadcasted_iota(jnp.int32, (TILE, TILE), 1).astype(F32)
    diff = ii - jj
    idx = lax.broadcasted_iota(jnp.int32, (TILE, 1), 0).astype(F32)
    for hh in range(RET_HEADS):
        sl = slice(hh * RET_DK, (hh + 1) * RET_DK)
        lg = _log_gamma(hh)
        decay = jnp.where(diff >= 0, jnp.exp(jnp.maximum(diff, 0.0) * lg), 0.0)
        q_dec = jnp.exp((idx + 1.0) * lg)
        k_decay = jnp.exp((TILE - 1.0 - idx) * lg)
        for bb in range(rq_ref.shape[0]):
            q = rq_ref[bb, :, sl]
            k = rk_ref[bb, :, sl]
            vb = rv_ref[bb, :, sl].astype(BF16)
            st = state[bb, hh]
            inner = _dot_nt(q.astype(BF16), k.astype(BF16)) * decay
            o = _dot(inner.astype(BF16), vb) + _dot((q * q_dec).astype(BF16), st.astype(BF16))
            state[bb, hh] = _exp32(TILE * lg) * st + _dot((k * k_decay).T.astype(BF16), vb)
            o_ref[bb, :, sl] = _group_norm(o) * gn_ref[:, sl]

    @pl.when(t == pl.num_programs(1) - 1)
    def _():
        st_ref[...] = state[...]


def _merge(xn, attn_proj, ron, w2, wro, wo, g1, b1):
    g3 = _dot(xn.astype(BF16), w2)
    rg = g3[:, :RET_W]
    ga = g3[:, RET_W:RET_W + D_MODEL]
    gb = g3[:, RET_W + D_MODEL:]
    ret = (ron * (rg * jax.nn.sigmoid(rg))).astype(BF16)
    m = jax.nn.sigmoid(ga) * attn_proj + jax.nn.sigmoid(gb) * _dot(ret, wro)
    return _layer_norm(ALPHA * xn + _dot(m.astype(BF16), wo), g1, b1)


def _ffn_tail(x1, a, a1, a2, g, cw, cb, wfo, g2, b2):
    acc = cb + cw[0:1] * a2 + cw[1:2] * a1 + cw[2:3] * a
    hid = (jax.nn.gelu(acc) * g).astype(BF16)
    return _layer_norm(ALPHA * x1 + _dot(hid, wfo), g2, b2)


def _merge_ffn_prompt_kernel(x_ref, ao_ref, ro_ref, g0, b0, w2, wao, wro, wo, g1, b1,
                             wfi, cw, cb, wfo, g2, b2, y_ref, cs_ref, carry, *, nb, plen):
    i = pl.program_id(0)
    tm = x_ref.shape[0]

    @pl.when(i == 0)
    def _():
        carry[...] = jnp.zeros_like(carry)

    _, pos_all = _position_rows(None, tm, plen)
    prev = carry[...]
    a_parts = []
    sub = min(MF_SUB, tm)
    for s in range(tm // sub):
        rows = slice(s * sub, (s + 1) * sub)
        xn = _layer_norm(x_ref[rows, :], g0[...], b0[...])
        attn_proj = None
        for pr in range(N_PAIRS):
            part = _dot(ao_ref[pr, rows, :], wao[pr * 2 * V_DIM:(pr + 1) * 2 * V_DIM, :])
            attn_proj = part if attn_proj is None else attn_proj + part
        x1 = _merge(xn, attn_proj, ro_ref[rows, :], w2[...], wro[...], wo[...], g1[...], b1[...])
        x1b = x1.astype(BF16)
        a = _dot(x1b, wfi[:, :D_FF])
        g = _dot(x1b, wfi[:, D_FF:])
        row = lax.broadcasted_iota(jnp.int32, (sub, 1), 0)
        pos = pos_all[rows]
        a1 = jnp.where(row == 0, prev[7:8], pltpu.roll(a, 1, 0))
        a2 = jnp.where(row == 0, prev[6:7], jnp.where(row == 1, prev[7:8], pltpu.roll(a, 2, 0)))
        a1 = jnp.where(pos >= FRONT_PAD + 1, a1, 0.0)
        a2 = jnp.where(pos >= FRONT_PAD + 2, a2, 0.0)
        prev = a[sub - 8:]
        y_ref[rows, :] = _ffn_tail(x1, a, a1, a2, g, cw[...], cb[...], wfo[...], g2[...], b2[...])
        a_parts.append(a)
    carry[...] = prev
    a = jnp.concatenate(a_parts, 0)

    for b in range(nb):
        end = (b + 1) * plen
        local = (end - 1) % tm + 1

        @pl.when(i == (end - 1) // tm)
        def _():
            cs_ref[b] = a[local - (CONV_W - 1):local]


def _merge_ffn_sample_kernel(x_ref, olat_ref, ro_ref, prev_ref, g0, b0, w2, wuv3, wao, wro, wo, g1, b1,
                             wfi, cw, cb, wfo, g2, b2, y_ref, cs_ref):
    xn = _layer_norm(x_ref[...], g0[...], b0[...])
    attn_proj = None
    for hh in range(MLA_HEADS):
        oh = _dot(olat_ref[hh].astype(BF16), wuv3[hh]).astype(BF16)
        part = _dot(oh, wao[hh * V_DIM:(hh + 1) * V_DIM, :])
        attn_proj = part if attn_proj is None else attn_proj + part
    x1 = _merge(xn, attn_proj, ro_ref[...], w2[...], wro[...], wo[...], g1[...], b1[...])
    x1b = x1.astype(BF16)
    a = _dot(x1b, wfi[:, :D_FF])
    g = _dot(x1b, wfi[:, D_FF:])
    prev0 = prev_ref[:, :D_FF]
    prev1 = prev_ref[:, D_FF:]
    y_ref[...] = _ffn_tail(x1, a, prev1, prev0, g, cw[...], cb[...], wfo[...], g2[...], b2[...])
    cs_ref[:, :D_FF] = prev1
    cs_ref[:, D_FF:] = a


def _attn_sample_kernel(pt_ref, qabs_ref, qr_ref, cnew_ref, krnew_ref, cache_c, cache_kr, o_ref,
                        cbuf, krbuf, cb16, sems, *, n_chunks):
    b = pl.program_id(0)
    nb = pl.num_programs(0)

    def copies(bb, ci, slot):
        out = []
        for p in range(PAGES_PER_CHUNK):
            page = pt_ref[bb, ci * PAGES_PER_CHUNK + p]
            rows = pl.ds(p * PAGE_SIZE, PAGE_SIZE)
            out.append((pltpu.make_async_copy(cache_c.at[0, page], cbuf.at[slot, rows], sems.at[0, slot]), p % 2))
            out.append((pltpu.make_async_copy(cache_kr.at[0, page], krbuf.at[slot, :, rows], sems.at[1, slot]),
                        (p + 1) % 2))
        return out

    def start(bb, ci, slot):
        for cp, prio in copies(bb, ci, slot):
            cp.start(priority=prio)

    ahead = DMA_SLOTS - 1
    ring = lambda g: lax.rem(g, DMA_SLOTS)

    @pl.when(b == 0)
    def _():
        for ci in range(ahead):
            start(0, ci, ci)

    qabs = qabs_ref[0]
    qr = qr_ref[0]
    m = jnp.full((MLA_HEADS, 1), NEG, F32)
    l = jnp.zeros((MLA_HEADS, 1), F32)
    acc = jnp.zeros((MLA_HEADS, KV_LORA), F32)
    pending = None
    for ci in range(n_chunks):
        g = b * n_chunks + ci
        slot = ring(g)
        half = ci % 2
        if ci + ahead < n_chunks:
            start(b, ci + ahead, ring(g + ahead))
        else:
            pl.when(b + 1 < nb)(functools.partial(start, b + 1, ci + ahead - n_chunks, ring(g + ahead)))
        for cp, _ in copies(b, ci, slot):
            cp.wait()
        cb16[half] = cbuf[slot].astype(BF16)
        s = _dot_nt(qabs, cb16[half]) + _dot(qr, krbuf[slot].astype(BF16))
        if pending is not None:
            acc = pending[1] * acc + _dot(pending[0], cb16[pending[2]])
        m_new = jnp.maximum(m, jnp.max(s, -1, keepdims=True))
        p = jnp.exp(s - m_new)
        alpha = jnp.exp(m - m_new)
        l = alpha * l + jnp.sum(p, -1, keepdims=True)
        m = m_new
        pending = (p.astype(BF16), alpha, half)
    acc = pending[1] * acc + _dot(pending[0], cb16[pending[2]])
    cnew = cnew_ref[0]
    s_new = (jnp.sum(qabs.astype(F32) * cnew, -1, keepdims=True)
             + jnp.sum(qr.astype(F32) * krnew_ref[0], -1, keepdims=True))
    m_new = jnp.maximum(m, s_new)
    p_new = jnp.exp(s_new - m_new)
    alpha = jnp.exp(m - m_new)
    o_ref[0] = (alpha * acc + p_new * cnew) / (alpha * l + p_new)


RS_ROWS = 8


def _ret_sample_kernel(rq_ref, rk_ref, rv_ref, st_ref, gn_ref, o_ref, ns_ref):
    for hh in range(RET_HEADS):
        sl = slice(hh * RET_DK, (hh + 1) * RET_DK)
        gamma = _exp32(_log_gamma(hh))
        q8 = rq_ref[:, sl]
        k8 = rk_ref[:, sl]
        v8 = rv_ref[:, sl]
        qk = jnp.sum(q8 * k8, -1, keepdims=True)
        for r in range(RS_ROWS):
            kcol = jnp.broadcast_to(k8[r:r + 1], (RET_DK, RET_DK)).T
            qcol = jnp.broadcast_to(q8[r:r + 1], (RET_DK, RET_DK)).T
            st = st_ref[r, hh]
            v = v8[r:r + 1]
            ns_ref[r, hh] = gamma * st + kcol * v
            o = qk[r:r + 1] * v + jnp.sum(qcol * gamma * st, 0, keepdims=True)
            o_ref[r:r + 1, sl] = _group_norm(o) * gn_ref[:, sl]


def _rope_tables(pos):
    n = pos.shape[0]

    def cos_sin(half):
        inv = ROPE_THETA ** (-jnp.arange(half, dtype=F32) / half)
        ang = pos[:, None] * inv[None, :]
        return jnp.cos(ang), jnp.sin(ang)

    c16, s16 = cos_sin(ROPE_DIM // 2)
    c64, s64 = cos_sin(RET_DK // 2)
    z = lambda w: jnp.zeros((n, w), F32)
    scale = (NOPE_DIM + ROPE_DIM) ** -0.5
    tail = LANES - NOPE_DIM - ROPE_DIM
    qc = jnp.concatenate([jnp.ones((n, NOPE_DIM), F32), c16, c16, z(tail)], 1) * scale
    qs1 = jnp.concatenate([z(NOPE_DIM + ROPE_DIM // 2), s16, z(tail)], 1) * scale
    qs2 = jnp.concatenate([z(NOPE_DIM), -s16, z(ROPE_DIM // 2 + tail)], 1) * scale
    rc = jnp.concatenate([c64, c64], 1)
    rs = jnp.concatenate([-s64, s64], 1)
    kc = jnp.concatenate([c16, c16, z(LANES - ROPE_DIM)], 1)
    ks = jnp.concatenate([-s16, s16, z(LANES - ROPE_DIM)], 1)
    return jnp.concatenate([qc, qs1, qs2, rc, rs, kc, ks], 1)


def _prep_weights(w_in, w_uq, w_uk, w_uv, w_ffn_in):
    sizes = (Q_LORA, KV_LORA, ROPE_DIM, RET_W, RET_W, RET_W, RET_W, D_MODEL, D_MODEL)
    offs = [0]
    for s in sizes:
        offs.append(offs[-1] + s)
    col = lambda i: w_in[:, offs[i]:offs[i + 1]]
    w_kr = col(2)
    half = ROPE_DIM // 2
    zpad = jnp.zeros((D_MODEL, LANES - ROPE_DIM), F32)
    w1 = jnp.concatenate([col(0), col(1), col(3), col(4), col(5), w_kr, zpad,
                          w_kr[:, half:], w_kr[:, :half], zpad], 1).astype(BF16)
    w2 = jnp.concatenate([col(6), col(7), col(8)], 1).astype(BF16)
    hd = NOPE_DIM + ROPE_DIM
    wuq = jnp.pad(w_uq, ((0, 0), (0, 0), (0, LANES - hd))).reshape(Q_LORA, HEAD_W).astype(BF16)
    wukp = jnp.pad(w_uk, ((0, 0), (0, 0), (0, LANES - NOPE_DIM))).reshape(KV_LORA, HEAD_W).astype(BF16)
    e1 = jnp.pad(jnp.eye(ROPE_DIM, dtype=F32), ((0, LANES - ROPE_DIM), (NOPE_DIM, LANES - hd)))
    e8 = jnp.tile(e1, (1, MLA_HEADS)).astype(BF16)
    wuv = w_uv.reshape(KV_LORA, MLA_HEADS * V_DIM).astype(BF16)
    wukt = jnp.pad(jnp.transpose(w_uk, (1, 2, 0)), ((0, 0), (0, LANES - NOPE_DIM), (0, 0))).astype(BF16)
    wuv3 = jnp.transpose(w_uv, (1, 0, 2)).astype(BF16)
    return w1, w2, wuq, wukp, e8, wuv, wukt, wuv3, w_ffn_in.astype(BF16)


def kernel(x_prompt, x_sample, cache_kv_latent, cache_k_rope, state_retention, state_ffn_conv, page_table, meta_tokens, ln0_g, ln0_b, w_in, q_norm_g, kv_norm_g, w_uq, w_uk, w_uv, ret_gn_g, w_attn_out, w_ret_out, w_o, ln1_g, ln1_b, w_ffn_in, conv_w, conv_b, w_ffn_out, ln2_g, ln2_b):
    assert w_in.shape[0] == DEPTH == 1
    nb, seq, _ = x_prompt.shape
    db, dec_seq, _ = x_sample.shape
    n_pages = page_table.shape[1]
    assert seq % TILE == 0 and dec_seq == 1 and db % RS_ROWS == 0
    assert n_pages % (2 * PAGES_PER_CHUNK) == 0 and n_pages // PAGES_PER_CHUNK >= DMA_SLOTS - 1
    nt = seq // TILE + 1
    plen = nt * TILE
    past = n_pages * PAGE_SIZE

    w1, w2, wuq, wukp, e8, wuv, wukt, wuv3, wfi = _prep_weights(w_in[0], w_uq[0], w_uk[0], w_uv[0], w_ffn_in[0])
    wao = w_attn_out[0].astype(BF16)
    wro = w_ret_out[0].astype(BF16)
    wo = w_o[0].astype(BF16)
    wfo = w_ffn_out[0].astype(BF16)
    row = lambda v: v.reshape(1, -1)
    g0, b0 = row(ln0_g), row(ln0_b)
    g1, b1, g2, b2 = row(ln1_g[0]), row(ln1_b[0]), row(ln2_g[0]), row(ln2_b[0])
    qg, kg, gn = row(q_norm_g[0]), row(kv_norm_g[0]), row(ret_gn_g[0])
    cw, cb = conv_w[0], row(conv_b[0])

    tab_p = _rope_tables(jnp.maximum(jnp.arange(plen, dtype=F32) - FRONT_PAD, 0.0))
    tab_s = jnp.broadcast_to(_rope_tables(jnp.full((1,), past, F32)), (db, N_TAB * LANES))
    n_rows = nb * plen
    xp = jnp.concatenate([jnp.zeros((nb, FRONT_PAD, D_MODEL), F32),
                          jnp.broadcast_to(meta_tokens.astype(F32)[None], (nb, N_META, D_MODEL)),
                          x_prompt], 1).reshape(n_rows, D_MODEL)
    in_tm = IN_TM if n_rows % IN_TM == 0 else TILE
    mf_tm = MF_TM if n_rows % MF_TM == 0 else TILE

    flat = lambda tm, w: pl.BlockSpec((tm, w), lambda i: (i, 0))
    pair_flat = lambda tm, w: pl.BlockSpec((N_PAIRS, tm, w), lambda i: (0, i, 0))
    pair_tok = lambda w: pl.BlockSpec((N_PAIRS, TILE, w), lambda b, t: (0, b * nt + t, 0))
    pair_seq = lambda w: pl.BlockSpec((N_PAIRS, plen, w), lambda b, t: (0, b, 0))
    wspec = _whole()
    sds = jax.ShapeDtypeStruct

    q_p, k_p, v_p, c_p, kr_p, rq_p, rk_p, rv_p = pl.pallas_call(
        functools.partial(_inproj_prompt_kernel, plen=plen),
        grid=(n_rows // in_tm,),
        in_specs=[flat(in_tm, D_MODEL)] + [wspec] * 10,
        out_specs=[pair_flat(in_tm, 2 * LANES), pair_flat(in_tm, 2 * LANES), pair_flat(in_tm, 2 * V_DIM),
                   flat(in_tm, KV_LORA), flat(in_tm, ROPE_DIM), flat(in_tm, RET_W), flat(in_tm, RET_W),
                   flat(in_tm, RET_W)],
        out_shape=[sds((N_PAIRS, n_rows, 2 * LANES), BF16), sds((N_PAIRS, n_rows, 2 * LANES), BF16),
                   sds((N_PAIRS, n_rows, 2 * V_DIM), BF16), sds((n_rows, KV_LORA), F32),
                   sds((n_rows, ROPE_DIM), F32), sds((n_rows, RET_W), F32),
                   sds((n_rows, RET_W), F32), sds((n_rows, RET_W), F32)],
        compiler_params=_params(1),
        name="inproj_prompt",
    )(xp, tab_p, g0, b0, w1, qg, kg, wuq, wukp, e8, wuv)

    bounds = tuple(sorted({-(-nt * c // ATTN_CLASSES) for c in range(1, ATTN_CLASSES + 1)}))
    ao_p = pl.pallas_call(
        functools.partial(_attn_prompt_kernel, bounds=bounds),
        grid=(nb, nt),
        in_specs=[pair_tok(2 * LANES), pair_seq(2 * LANES), pair_seq(2 * V_DIM)],
        out_specs=pair_tok(2 * V_DIM),
        out_shape=sds((N_PAIRS, n_rows, 2 * V_DIM), BF16),
        compiler_params=_params(2),
        name="attn_prompt",
    )(q_p, k_p, v_p)

    seq3 = lambda v: v.reshape(nb, plen, v.shape[-1])
    rb = RET_SEQS if nb % RET_SEQS == 0 else 1
    tok = lambda w: pl.BlockSpec((rb, TILE, w), lambda b, t: (b, t, 0))
    ro_p, st_p = pl.pallas_call(
        _ret_prompt_kernel,
        grid=(nb // rb, nt),
        in_specs=[tok(RET_W), tok(RET_W), tok(RET_W), wspec],
        out_specs=[tok(RET_W), pl.BlockSpec((rb, RET_HEADS, RET_DK, RET_DV), lambda b, t: (b, 0, 0, 0))],
        out_shape=[sds((nb, plen, RET_W), F32), sds((nb, RET_HEADS, RET_DK, RET_DV), F32)],
        scratch_shapes=[pltpu.VMEM((rb, RET_HEADS, RET_DK, RET_DV), F32)],
        compiler_params=_params(2),
        name="ret_prompt",
    )(seq3(rq_p), seq3(rk_p), seq3(rv_p), gn)

    y_pad, cs_p = pl.pallas_call(
        functools.partial(_merge_ffn_prompt_kernel, nb=nb, plen=plen),
        grid=(n_rows // mf_tm,),
        in_specs=[flat(mf_tm, D_MODEL), pair_flat(mf_tm, 2 * V_DIM), flat(mf_tm, RET_W)] + [wspec] * 14,
        out_specs=[flat(mf_tm, D_MODEL), wspec],
        out_shape=[sds((n_rows, D_MODEL), F32), sds((nb, CONV_W - 1, D_FF), F32)],
        scratch_shapes=[pltpu.VMEM((8, D_FF), F32)],
        compiler_params=_params(1),
        name="merge_ffn_prompt",
    )(xp, ao_p, ro_p.reshape(n_rows, RET_W), g0, b0, w2, wao, wro, wo, g1, b1, wfi, cw, cb, wfo, g2, b2)
    y_p = y_pad.reshape(nb, plen, D_MODEL)[:, TILE:]
    c_p = seq3(c_p)
    kr_p = seq3(kr_p)

    xs = x_sample.reshape(db, D_MODEL)
    q_s, qabs_s, c_s, kr_s, rq_s, rk_s, rv_s = pl.pallas_call(
        _inproj_sample_kernel,
        in_specs=[wspec] * 9,
        out_specs=[wspec] * 7,
        out_shape=[sds((db, HEAD_W), BF16), sds((MLA_HEADS, db, KV_LORA), BF16), sds((db, KV_LORA), F32),
                   sds((db, ROPE_DIM), F32), sds((db, RET_W), F32), sds((db, RET_W), F32), sds((db, RET_W), F32)],
        compiler_params=pltpu.CompilerParams(vmem_limit_bytes=VMEM_LIMIT),
        name="inproj_sample",
    )(xs, tab_s, g0, b0, w1, qg, kg, wuq, wukt)

    qabs_t = jnp.transpose(qabs_s, (1, 0, 2))
    qr_s = q_s.reshape(db, MLA_HEADS, LANES)[:, :, NOPE_DIM:NOPE_DIM + ROPE_DIM]
    n_chunks = n_pages // PAGES_PER_CHUNK
    chunk_rows = PAGES_PER_CHUNK * PAGE_SIZE
    per_req = lambda s1, s2: pl.BlockSpec((1, s1, s2), lambda b, pt: (b, 0, 0))
    olat_s = pl.pallas_call(
        functools.partial(_attn_sample_kernel, n_chunks=n_chunks),
        grid_spec=pltpu.PrefetchScalarGridSpec(
            num_scalar_prefetch=1,
            grid=(db,),
            in_specs=[per_req(MLA_HEADS, KV_LORA), per_req(MLA_HEADS, ROPE_DIM), per_req(1, KV_LORA),
                      per_req(1, ROPE_DIM), _whole(pl.ANY), _whole(pl.ANY)],
            out_specs=per_req(MLA_HEADS, KV_LORA),
            scratch_shapes=[pltpu.VMEM((DMA_SLOTS, chunk_rows, KV_LORA), F32),
                            pltpu.VMEM((DMA_SLOTS, ROPE_DIM, chunk_rows), F32),
                            pltpu.VMEM((2, chunk_rows, KV_LORA), BF16),
                            pltpu.SemaphoreType.DMA((2, DMA_SLOTS))]),
        out_shape=sds((db, MLA_HEADS, KV_LORA), F32),
        compiler_params=_params(1),
        name="attn_sample",
    )(page_table, qabs_t, qr_s, c_s.reshape(db, 1, KV_LORA), kr_s.reshape(db, 1, ROPE_DIM),
      cache_kv_latent, jnp.swapaxes(cache_k_rope, 2, 3))

    rows = lambda w: pl.BlockSpec((RS_ROWS, w), lambda i: (i, 0))
    st_spec = pl.BlockSpec((RS_ROWS, RET_HEADS, RET_DK, RET_DV), lambda i: (i, 0, 0, 0))
    ro_s, st_s = pl.pallas_call(
        _ret_sample_kernel,
        grid=(db // RS_ROWS,),
        in_specs=[rows(RET_W), rows(RET_W), rows(RET_W), st_spec, wspec],
        out_specs=[rows(RET_W), st_spec],
        out_shape=[sds((db, RET_W), F32), sds((db, RET_HEADS, RET_DK, RET_DV), F32)],
        compiler_params=_params(1),
        name="ret_sample",
    )(rq_s, rk_s, rv_s, state_retention.reshape(db, RET_HEADS, RET_DK, RET_DV), gn)

    y_s, cs_s = pl.pallas_call(
        _merge_ffn_sample_kernel,
        in_specs=[wspec] * 19,
        out_specs=[wspec] * 2,
        out_shape=[sds((db, D_MODEL), F32), sds((db, (CONV_W - 1) * D_FF), F32)],
        compiler_params=pltpu.CompilerParams(vmem_limit_bytes=VMEM_LIMIT),
        name="merge_ffn_sample",
    )(xs, jnp.transpose(olat_s, (1, 0, 2)), ro_s, state_ffn_conv.reshape(db, (CONV_W - 1) * D_FF),
      g0, b0, w2, wuv3, wao, wro, wo, g1, b1, wfi, cw, cb, wfo, g2, b2)

    first = FRONT_PAD
    return (y_p,
            y_s.reshape(db, 1, D_MODEL),
            c_p[None, :, first:],
            kr_p[None, :, first:],
            st_p[None],
            cs_p[None],
            c_s.reshape(1, db, 1, KV_LORA),
            kr_s.reshape(1, db, 1, ROPE_DIM),
            st_s[None],
            cs_s.reshape(1, db, CONV_W - 1, D_FF))
```

```python
import functools

import numpy as np
import jax
import jax.numpy as jnp
from jax import lax
from jax.experimental import pallas as pl
from jax.experimental.pallas import tpu as pltpu

F32 = jnp.float32
BF16 = jnp.bfloat16

D_MODEL = 1024
N_META = 16
MLA_HEADS = 8
Q_LORA = 384
KV_LORA = 256
NOPE_DIM = 64
ROPE_DIM = 32
V_DIM = 64
ROPE_THETA = 10000.0
RET_HEADS = 4
RET_DK = 128
RET_DV = 128
D_FF = 2816
CONV_W = 3
PAGE_SIZE = 128
LN_EPS = 1e-5
RMS_EPS = 1e-6
DEPTH = 1
ALPHA = (2 * DEPTH) ** 0.25

LANES = 128
TILE = 128
FRONT_PAD = TILE - N_META
RET_W = RET_HEADS * RET_DK
HEAD_W = MLA_HEADS * LANES
N_PAIRS = MLA_HEADS // 2
ATTN_CLASSES = 6
N_TAB = 7
W1_COLS = Q_LORA + KV_LORA + 3 * RET_W + 2 * LANES
W2_COLS = RET_W + 2 * D_MODEL
NEG = -1e30
PAGES_PER_CHUNK = 32
IN_TM = 512
IN_SUB = 512
MF_TM = 512
MF_SUB = 256
RET_SEQS = 4
DMA_SLOTS = 4
VMEM_LIMIT = 56 * 1024 * 1024


def _dot(a, b):
    return jnp.dot(a, b, preferred_element_type=F32)


def _dot_nt(a, b):
    return lax.dot_general(a, b, (((1,), (1,)), ((), ())), preferred_element_type=F32)


def _layer_norm(x, g, b):
    mu = jnp.mean(x, -1, keepdims=True)
    xc = x - mu
    var = jnp.mean(xc * xc, -1, keepdims=True)
    return xc * lax.rsqrt(var + LN_EPS) * g + b


def _rms_norm(x, g):
    return x * lax.rsqrt(jnp.mean(x * x, -1, keepdims=True) + RMS_EPS) * g


def _group_norm(o):
    mu = jnp.mean(o, -1, keepdims=True)
    oc = o - mu
    var = jnp.mean(oc * oc, -1, keepdims=True)
    return oc * lax.rsqrt(var + LN_EPS)


def _log_gamma(h):
    return float(np.log(np.float32(1.0) - np.float32(2.0) ** np.float32(-5.0 - h), dtype=np.float32))


def _exp32(v):
    return float(np.exp(np.float32(v), dtype=np.float32))


def _whole(memory_space=pltpu.VMEM):
    return pl.BlockSpec(memory_space=memory_space)


def _params(n_axes):
    return pltpu.CompilerParams(dimension_semantics=("arbitrary",) * n_axes, vmem_limit_bytes=VMEM_LIMIT)


def _inproj_common(x, tab, g0, b0, w1, qg, kg, wuq):
    xn = _layer_norm(x, g0, b0)
    h = _dot(xn.astype(BF16), w1)
    o = 0
    q_lat = h[:, o:o + Q_LORA]; o += Q_LORA
    c_raw = h[:, o:o + KV_LORA]; o += KV_LORA
    rq = h[:, o:o + RET_W]; o += RET_W
    rk = h[:, o:o + RET_W]; o += RET_W
    rv = h[:, o:o + RET_W]; o += RET_W
    kr = h[:, o:o + LANES]; o += LANES
    krs = h[:, o:o + LANES]
    qc, qs1, qs2, rc, rs, kc, ks = [tab[:, i * LANES:(i + 1) * LANES] for i in range(N_TAB)]
    c = _rms_norm(c_raw, kg)
    krope = kr * kc + krs * ks
    q = _dot(_rms_norm(q_lat, qg).astype(BF16), wuq)
    q_heads = []
    for hh in range(MLA_HEADS):
        qh = q[:, hh * LANES:(hh + 1) * LANES]
        q_heads.append(qh * qc + pltpu.roll(qh, ROPE_DIM // 2, 1) * qs1
                       + pltpu.roll(qh, LANES - ROPE_DIM // 2, 1) * qs2)

    def rope_ret(x):
        return [x[:, hh * RET_DK:(hh + 1) * RET_DK] * rc
                + pltpu.roll(x[:, hh * RET_DK:(hh + 1) * RET_DK], RET_DK // 2, 1) * rs
                for hh in range(RET_HEADS)]

    rq_h = rope_ret(rq)
    rk_h = [v * (RET_DK ** -0.5) for v in rope_ret(rk)]
    return q_heads, c, krope, rq_h, rk_h, rv


def _prompt_rows(x_refs, meta_ref, tab_ref, first_piece, n_pieces, nt):
    row = lax.broadcasted_iota(jnp.int32, (TILE, 1), 0)
    xs, tabs, pos = [], [], []
    for j in range(first_piece, first_piece + n_pieces):
        t = (pl.program_id(0) * len(x_refs) + j) % nt
        xs.append(jnp.where(t == 0, meta_ref[...], x_refs[j][0]))
        p0 = pl.multiple_of(t * TILE, TILE)
        if tab_ref is not None:
            tabs.append(tab_ref[pl.ds(p0, TILE), :])
        pos.append(p0 + row)
    cat = lambda parts: parts[0] if len(parts) == 1 else jnp.concatenate(parts, 0)
    return cat(xs), (cat(tabs) if tabs else None), cat(pos)


def _inproj_prompt_kernel(*refs, nt, n_x):
    x_refs = refs[:n_x]
    (meta_ref, tab_ref, g0, b0, w1, qg, kg, wuq, wukp, e8, wuv,
     q_o, k_o, v_o, c_o, kr_o, rq_o, rk_o, rv_o) = refs[n_x:]
    per_group = min(IN_SUB // TILE, n_x)
    for s in range(n_x // per_group):
        rows = slice(s * per_group * TILE, (s + 1) * per_group * TILE)
        x, tab, pos = _prompt_rows(x_refs, meta_ref, tab_ref, s * per_group, per_group, nt)
        real = pos >= FRONT_PAD
        q_heads, c, krope, rq_h, rk_h, rv = _inproj_common(
            x, tab, g0[...], b0[...], w1[...], qg[...], kg[...], wuq[...])
        for hh in range(MLA_HEADS):
            q_o[hh // 2, rows, (hh % 2) * LANES:(hh % 2 + 1) * LANES] = q_heads[hh].astype(BF16)
        for hh in range(RET_HEADS):
            sl = slice(hh * RET_DK, (hh + 1) * RET_DK)
            rq_o[rows, sl] = rq_h[hh]
            rk_o[rows, sl] = jnp.where(real, rk_h[hh], 0.0)
        rv_o[rows, :] = rv
        cb = c.astype(BF16)
        k = (_dot(cb, wukp[...]) + _dot(krope.astype(BF16), e8[...])).astype(BF16)
        v = _dot(cb, wuv[...]).astype(BF16)
        for pr in range(N_PAIRS):
            k_o[pr, rows, :] = k[:, pr * 2 * LANES:(pr + 1) * 2 * LANES]
            v_o[pr, rows, :] = v[:, pr * 2 * V_DIM:(pr + 1) * 2 * V_DIM]
        c_o[rows, :] = c
        kr_o[rows, :] = krope[:, :ROPE_DIM]


def _inproj_sample_kernel(x_ref, tab_ref, g0, b0, w1, qg, kg, wuq, wukt,
                          q_o, qabs_o, c_o, kr_o, rq_o, rk_o, rv_o):
    q_heads, c, krope, rq_h, rk_h, rv = _inproj_common(
        x_ref[...], tab_ref[...], g0[...], b0[...], w1[...], qg[...], kg[...], wuq[...])
    for hh in range(MLA_HEADS):
        qb = q_heads[hh].astype(BF16)
        q_o[:, hh * LANES:(hh + 1) * LANES] = qb
        qabs_o[hh] = _dot(qb, wukt[hh]).astype(BF16)
    for hh in range(RET_HEADS):
        sl = slice(hh * RET_DK, (hh + 1) * RET_DK)
        rq_o[:, sl] = rq_h[hh]
        rk_o[:, sl] = rk_h[hh]
    rv_o[...] = rv
    c_o[...] = c
    kr_o[...] = krope[:, :ROPE_DIM]


def _attn_prompt_kernel(q_ref, k_ref, v_ref, o_ref, *, bounds):
    qi = pl.program_id(1)
    row = lax.broadcasted_iota(jnp.int32, (TILE, TILE), 0) + qi * TILE
    lane = lax.broadcasted_iota(jnp.int32, (TILE, TILE), 1)
    first = lane < V_DIM

    def run(lo, wt):
        width = wt * TILE

        def pair_body(pr, carry):
            q = q_ref[pr]
            k = k_ref[pr, :width, :]
            v = v_ref[pr, :width, :]
            outs = []
            for hh in range(2):
                s = _dot_nt(q[:, hh * LANES:(hh + 1) * LANES], k[:, hh * LANES:(hh + 1) * LANES])
                tiles = []
                for c in range(wt):
                    sc = s[:, c * TILE:(c + 1) * TILE]
                    if c == 0:
                        sc = jnp.where(lane >= FRONT_PAD, sc, NEG)
                    if c >= lo:
                        sc = jnp.where(lane + c * TILE <= row, sc, NEG)
                    tiles.append(sc)
                s = jnp.concatenate(tiles, 1)
                p = jnp.exp(s - jnp.max(s, -1, keepdims=True))
                inv_l = 1.0 / jnp.sum(p, -1, keepdims=True)
                outs.append(_dot(p.astype(BF16), v) * inv_l)
            o_ref[pr] = jnp.where(first, outs[0], outs[1]).astype(BF16)
            return carry

        lax.fori_loop(0, N_PAIRS, pair_body, 0, unroll=True)

    lo = 0
    for wt in bounds:
        pl.when((qi >= lo) & (qi < wt))(functools.partial(run, lo, wt))
        lo = wt


def _ret_prompt_kernel(rq_ref, rk_ref, rv_ref, gn_ref, o_ref, st_ref, state):
    t = pl.program_id(1)

    @pl.when(t == 0)
    def _():
        state[...] = jnp.zeros_like(state)

    ii = lax.broadcasted_iota(jnp.int32, (TILE, TILE), 0).astype(F32)
    jj = lax.broadcasted_iota(jnp.int32, (TILE, TILE), 1).astype(F32)
    diff = ii - jj
    idx = lax.broadcasted_iota(jnp.int32, (TILE, 1), 0).astype(F32)
    for hh in range(RET_HEADS):
        sl = slice(hh * RET_DK, (hh + 1) * RET_DK)
        lg = _log_gamma(hh)
        decay = jnp.where(diff >= 0, jnp.exp(jnp.maximum(diff, 0.0) * lg), 0.0)
        q_dec = jnp.exp((idx + 1.0) * lg)
        k_decay = jnp.exp((TILE - 1.0 - idx) * lg)
        for bb in range(rq_ref.shape[0]):
            q = rq_ref[bb, :, sl]
            k = rk_ref[bb, :, sl]
            vb = rv_ref[bb, :, sl].astype(BF16)
            st = state[bb, hh]
            inner = _dot_nt(q.astype(BF16), k.astype(BF16)) * decay
            o = _dot(inner.astype(BF16), vb) + _dot((q * q_dec).astype(BF16), st.astype(BF16))
            state[bb, hh] = _exp32(TILE * lg) * st + _dot((k * k_decay).T.astype(BF16), vb)
            o_ref[bb, :, sl] = _group_norm(o) * gn_ref[:, sl]

    @pl.when(t == pl.num_programs(1) - 1)
    def _():
        st_ref[...] = state[...]


def _merge(xn, attn_proj, ron, w2, wro, wo, g1, b1):
    g3 = _dot(xn.astype(BF16), w2)
    rg = g3[:, :RET_W]
    ga = g3[:, RET_W:RET_W + D_MODEL]
    gb = g3[:, RET_W + D_MODEL:]
    ret = (ron * (rg * jax.nn.sigmoid(rg))).astype(BF16)
    m = jax.nn.sigmoid(ga) * attn_proj + jax.nn.sigmoid(gb) * _dot(ret, wro)
    return _layer_norm(ALPHA * xn + _dot(m.astype(BF16), wo), g1, b1)


def _ffn_tail(x1, a, a1, a2, g, cw, cb, wfo, g2, b2):
    acc = cb + cw[0:1] * a2 + cw[1:2] * a1 + cw[2:3] * a
    hid = (jax.nn.gelu(acc) * g).astype(BF16)
    return _layer_norm(ALPHA * x1 + _dot(hid, wfo), g2, b2)


def _merge_ffn_prompt_kernel(*refs, nb, nt, n_x):
    x_refs = refs[:n_x]
    (meta_ref, ao_ref, ro_ref, g0, b0, w2, wao, wro, wo, g1, b1,
     wfi, cw, cb, wfo, g2, b2, y_ref, cs_ref, carry) = refs[n_x:]
    i = pl.program_id(0)
    tm = n_x * TILE
    plen = nt * TILE

    @pl.when(i == 0)
    def _():
        carry[...] = jnp.zeros_like(carry)

    prev = carry[...]
    a_parts = []
    sub = min(MF_SUB, tm)
    for s in range(tm // sub):
        rows = slice(s * sub, (s + 1) * sub)
        x, _, pos = _prompt_rows(x_refs, meta_ref, None, s * (sub // TILE), sub // TILE, nt)
        xn = _layer_norm(x, g0[...], b0[...])
        attn_proj = None
        for pr in range(N_PAIRS):
            part = _dot(ao_ref[pr, rows, :], wao[pr * 2 * V_DIM:(pr + 1) * 2 * V_DIM, :])
            attn_proj = part if attn_proj is None else attn_proj + part
        x1 = _merge(xn, attn_proj, ro_ref[rows, :], w2[...], wro[...], wo[...], g1[...], b1[...])
        x1b = x1.astype(BF16)
        a = _dot(x1b, wfi[:, :D_FF])
        g = _dot(x1b, wfi[:, D_FF:])
        row = lax.broadcasted_iota(jnp.int32, (sub, 1), 0)
        a1 =jnp.where(row == 0, prev[7:8], pltpu.roll(a, 1, 0))
        a2 = jnp.where(row == 0, prev[6:7], jnp.where(row == 1, prev[7:8], pltpu.roll(a, 2, 0)))
        a1 = jnp.where(pos >= FRONT_PAD + 1, a1, 0.0)
        a2 = jnp.where(pos >= FRONT_PAD + 2, a2, 0.0)
        prev = a[sub - 8:]
        y_ref[rows, :] = _ffn_tail(x1, a, a1, a2, g, cw[...], cb[...], wfo[...], g2[...], b2[...])
        a_parts.append(a)
    carry[...] = prev
    a = jnp.concatenate(a_parts, 0)

    for b in range(nb):
        end = (b + 1) * plen
        local = (end - 1) % tm + 1

        @pl.when(i == (end - 1) // tm)
        def _():
            cs_ref[b] = a[local - (CONV_W - 1):local]


def _merge_ffn_sample_kernel(x_ref, olat_ref, ro_ref, prev_ref, g0, b0, w2, wuv3, wao, wro, wo, g1, b1,
                             wfi, cw, cb, wfo, g2, b2, y_ref, cs_ref):
    xn = _layer_norm(x_ref[...], g0[...], b0[...])
    attn_proj = None
    for hh in range(MLA_HEADS):
        oh = _dot(olat_ref[hh].astype(BF16), wuv3[hh]).astype(BF16)
        part = _dot(oh, wao[hh * V_DIM:(hh + 1) * V_DIM, :])
        attn_proj = part if attn_proj is None else attn_proj + part
    x1 = _merge(xn, attn_proj, ro_ref[...], w2[...], wro[...], wo[...], g1[...], b1[...])
    x1b = x1.astype(BF16)
    a = _dot(x1b, wfi[:, :D_FF])
    g = _dot(x1b, wfi[:, D_FF:])
    prev0 = prev_ref[:, :D_FF]
    prev1 = prev_ref[:, D_FF:]
    y_ref[...] = _ffn_tail(x1, a, prev1, prev0, g, cw[...], cb[...], wfo[...], g2[...], b2[...])
    cs_ref[:, :D_FF] = prev1
    cs_ref[:, D_FF:] = a


def _attn_sample_kernel(pt_ref, qabs_ref, qr_ref, cnew_ref, krnew_ref, cache_c, cache_kr, o_ref,
                        cbuf, krbuf, cb16, sems, *, n_chunks):
    b = pl.program_id(0)
    nb = pl.num_programs(0)

    def copies(bb, ci, slot):
        out = []
        for p in range(PAGES_PER_CHUNK):
            page = pt_ref[bb, ci * PAGES_PER_CHUNK + p]
            rows = pl.ds(p * PAGE_SIZE, PAGE_SIZE)
            out.append((pltpu.make_async_copy(cache_c.at[0, page], cbuf.at[slot, rows], sems.at[0, slot]), p % 2))
            out.append((pltpu.make_async_copy(cache_kr.at[0, page], krbuf.at[slot, :, rows], sems.at[1, slot]),
                        (p + 1) % 2))
        return out

    def start(bb, ci, slot):
        for cp, prio in copies(bb, ci, slot):
            cp.start(priority=prio)

    ahead = DMA_SLOTS - 1
    ring = lambda g: lax.rem(g, DMA_SLOTS)

    @pl.when(b == 0)
    def _():
        for g0 in range(ahead):
            start(g0 // n_chunks, g0 % n_chunks, g0)

    qabs = qabs_ref[0]
    qr = qr_ref[0]
    m = jnp.full((MLA_HEADS, 1), NEG, F32)
    l = jnp.zeros((MLA_HEADS, 1), F32)
    acc = jnp.zeros((MLA_HEADS, KV_LORA), F32)
    pending = None
    for ci in range(n_chunks):
        g = b * n_chunks + ci
        slot = ring(g)
        half = ci % 2
        later, ci_ahead = divmod(ci + ahead, n_chunks)
        if later == 0:
            start(b, ci_ahead, ring(g + ahead))
        else:
            pl.when(b + later < nb)(functools.partial(start, b + later, ci_ahead, ring(g + ahead)))
        for cp, _ in copies(b, ci, slot):
            cp.wait()
        cb16[half] = cbuf[slot].astype(BF16)
        s = _dot_nt(qabs, cb16[half]) + _dot(qr, krbuf[slot].astype(BF16))
        if pending is not None:
            acc = pending[1] * acc + _dot(pending[0], cb16[pending[2]])
        m_new = jnp.maximum(m, jnp.max(s, -1, keepdims=True))
        p = jnp.exp(s - m_new)
        alpha = jnp.exp(m - m_new)
        l = alpha * l + jnp.sum(p, -1, keepdims=True)
        m = m_new
        pending = (p.astype(BF16), alpha, half)
    acc = pending[1] * acc + _dot(pending[0], cb16[pending[2]])
    cnew = cnew_ref[0]
    s_new = (jnp.sum(qabs.astype(F32) * cnew, -1, keepdims=True)
             + jnp.sum(qr.astype(F32) * krnew_ref[0], -1, keepdims=True))
    m_new = jnp.maximum(m, s_new)
    p_new = jnp.exp(s_new - m_new)
    alpha = jnp.exp(m - m_new)
    o_ref[0] = (alpha * acc + p_new * cnew) / (alpha * l + p_new)


RS_ROWS = 8


def _ret_sample_kernel(rq_ref, rk_ref, rv_ref, st_ref, gn_ref, o_ref, ns_ref):
    for hh in range(RET_HEADS):
        sl = slice(hh * RET_DK, (hh + 1) * RET_DK)
        gamma = _exp32(_log_gamma(hh))
        q8 = rq_ref[:, sl]
        k8 = rk_ref[:, sl]
        v8 = rv_ref[:, sl]
        qk = jnp.sum(q8 * k8, -1, keepdims=True)
        for r in range(RS_ROWS):
            kcol = jnp.broadcast_to(k8[r:r + 1], (RET_DK, RET_DK)).T
            qcol = jnp.broadcast_to(q8[r:r + 1], (RET_DK, RET_DK)).T
            st = st_ref[r, hh]
            v = v8[r:r + 1]
            ns_ref[r, hh] = gamma * st + kcol * v
            o = qk[r:r + 1] * v + jnp.sum(qcol * gamma * st, 0, keepdims=True)
            o_ref[r:r + 1, sl] = _group_norm(o) * gn_ref[:, sl]


def _rope_tables(pos):
    n = pos.shape[0]

    def cos_sin(half):
        inv = ROPE_THETA ** (-jnp.arange(half, dtype=F32) / half)
        ang = pos[:, None] * inv[None, :]
        return jnp.cos(ang), jnp.sin(ang)

    c16, s16 = cos_sin(ROPE_DIM // 2)
    c64, s64 = cos_sin(RET_DK // 2)
    z = lambda w: jnp.zeros((n, w), F32)
    scale = (NOPE_DIM + ROPE_DIM) ** -0.5
    tail = LANES - NOPE_DIM - ROPE_DIM
    qc = jnp.concatenate([jnp.ones((n, NOPE_DIM), F32), c16, c16, z(tail)], 1) * scale
    qs1 = jnp.concatenate([z(NOPE_DIM + ROPE_DIM // 2), s16, z(tail)], 1) * scale
    qs2 = jnp.concatenate([z(NOPE_DIM), -s16, z(ROPE_DIM // 2 + tail)], 1) * scale
    rc = jnp.concatenate([c64, c64], 1)
    rs = jnp.concatenate([-s64, s64], 1)
    kc = jnp.concatenate([c16, c16, z(LANES - ROPE_DIM)], 1)
    ks = jnp.concatenate([-s16, s16, z(LANES - ROPE_DIM)], 1)
    return jnp.concatenate([qc, qs1, qs2, rc, rs, kc, ks], 1)


def _prep_weights(w_in, w_uq, w_uk, w_uv, w_ffn_in):
    sizes = (Q_LORA, KV_LORA, ROPE_DIM, RET_W, RET_W, RET_W, RET_W, D_MODEL, D_MODEL)
    offs = [0]
    for s in sizes:
        offs.append(offs[-1] + s)
    col = lambda i: w_in[:, offs[i]:offs[i + 1]]
    w_kr = col(2)
    half = ROPE_DIM // 2
    zpad = jnp.zeros((D_MODEL, LANES - ROPE_DIM), F32)
    w1 = jnp.concatenate([col(0), col(1), col(3), col(4), col(5), w_kr, zpad,
                          w_kr[:, half:], w_kr[:, :half], zpad], 1).astype(BF16)
    w2 = jnp.concatenate([col(6), col(7), col(8)], 1).astype(BF16)
    hd = NOPE_DIM + ROPE_DIM
    wuq = jnp.pad(w_uq, ((0, 0), (0, 0), (0, LANES - hd))).reshape(Q_LORA, HEAD_W).astype(BF16)
    wukp = jnp.pad(w_uk, ((0, 0), (0, 0), (0, LANES - NOPE_DIM))).reshape(KV_LORA, HEAD_W).astype(BF16)
    e1 = jnp.pad(jnp.eye(ROPE_DIM, dtype=F32), ((0, LANES - ROPE_DIM), (NOPE_DIM, LANES - hd)))
    e8 = jnp.tile(e1, (1, MLA_HEADS)).astype(BF16)
    wuv = w_uv.reshape(KV_LORA, MLA_HEADS * V_DIM).astype(BF16)
    wukt = jnp.pad(jnp.transpose(w_uk, (1, 2, 0)), ((0, 0), (0, LANES - NOPE_DIM), (0, 0))).astype(BF16)
    wuv3 = jnp.transpose(w_uv, (1, 0, 2)).astype(BF16)
    return w1, w2, wuq, wukp, e8, wuv, wukt, wuv3, w_ffn_in.astype(BF16)


def kernel(x_prompt, x_sample, cache_kv_latent, cache_k_rope, state_retention, state_ffn_conv, page_table, meta_tokens, ln0_g, ln0_b, w_in, q_norm_g, kv_norm_g, w_uq, w_uk, w_uv, ret_gn_g, w_attn_out, w_ret_out, w_o, ln1_g, ln1_b, w_ffn_in, conv_w, conv_b, w_ffn_out, ln2_g, ln2_b):
    assert w_in.shape[0] == DEPTH == 1
    nb, seq, _ = x_prompt.shape
    db, dec_seq, _ = x_sample.shape
    n_pages = page_table.shape[1]
    assert seq % TILE == 0 and dec_seq == 1 and db % RS_ROWS == 0
    assert n_pages % (2 * PAGES_PER_CHUNK) == 0 and db * (n_pages // PAGES_PER_CHUNK) >= DMA_SLOTS - 1
    nt = seq // TILE + 1
    plen = nt * TILE
    past = n_pages * PAGE_SIZE

    w1, w2, wuq, wukp, e8, wuv, wukt, wuv3, wfi = _prep_weights(w_in[0], w_uq[0], w_uk[0], w_uv[0], w_ffn_in[0])
    wao = w_attn_out[0].astype(BF16)
    wro = w_ret_out[0].astype(BF16)
    wo = w_o[0].astype(BF16)
    wfo = w_ffn_out[0].astype(BF16)
    row = lambda v: v.reshape(1, -1)
    g0, b0 = row(ln0_g), row(ln0_b)
    g1, b1, g2, b2 = row(ln1_g[0]), row(ln1_b[0]), row(ln2_g[0]), row(ln2_b[0])
    qg, kg, gn = row(q_norm_g[0]), row(kv_norm_g[0]), row(ret_gn_g[0])
    cw, cb = conv_w[0], row(conv_b[0])

    tab_p = _rope_tables(jnp.maximum(jnp.arange(plen, dtype=F32) - FRONT_PAD, 0.0))
    tab_s = jnp.broadcast_to(_rope_tables(jnp.full((1,), past, F32)), (db, N_TAB * LANES))
    n_rows = nb * plen
    meta_tile = jnp.concatenate([jnp.zeros((FRONT_PAD, D_MODEL), F32), meta_tokens.astype(F32)], 0)
    in_tm = IN_TM if n_rows % IN_TM == 0 else TILE
    mf_tm = MF_TM if n_rows % MF_TM == 0 else TILE

    def x_specs(tm):
        def spec(j):
            def index(i):
                g = i * (tm // TILE) + j
                return g // nt, jnp.maximum(g % nt - 1, 0), 0
            return pl.BlockSpec((1, TILE, D_MODEL), index)
        return [spec(j) for j in range(tm // TILE)]

    flat = lambda tm, w: pl.BlockSpec((tm, w), lambda i: (i, 0))
    pair_flat = lambda tm, w: pl.BlockSpec((N_PAIRS, tm, w), lambda i: (0, i, 0))
    pair_tok = lambda w: pl.BlockSpec((N_PAIRS, TILE, w), lambda b, t: (0, b * nt + t, 0))
    pair_seq = lambda w: pl.BlockSpec((N_PAIRS, plen, w), lambda b, t: (0, b, 0))
    wspec = _whole()
    sds = jax.ShapeDtypeStruct

    q_p, k_p, v_p, c_p, kr_p, rq_p, rk_p, rv_p = pl.pallas_call(
        functools.partial(_inproj_prompt_kernel, nt=nt, n_x=in_tm // TILE),
        grid=(n_rows // in_tm,),
        in_specs=x_specs(in_tm) + [wspec] * 11,
        out_specs=[pair_flat(in_tm, 2 * LANES), pair_flat(in_tm, 2 * LANES), pair_flat(in_tm, 2 * V_DIM),
                   flat(in_tm, KV_LORA), flat(in_tm, ROPE_DIM), flat(in_tm, RET_W), flat(in_tm, RET_W),
                   flat(in_tm, RET_W)],
        out_shape=[sds((N_PAIRS, n_rows, 2 * LANES), BF16), sds((N_PAIRS, n_rows, 2 * LANES), BF16),
                   sds((N_PAIRS, n_rows, 2 * V_DIM), BF16), sds((n_rows, KV_LORA), F32),
                   sds((n_rows, ROPE_DIM), F32), sds((n_rows, RET_W), F32),
                   sds((n_rows, RET_W), F32), sds((n_rows, RET_W), F32)],
        compiler_params=_params(1),
        name="inproj_prompt",
    )(*([x_prompt] * (in_tm // TILE)), meta_tile, tab_p, g0, b0, w1, qg, kg, wuq, wukp, e8, wuv)

    bounds = tuple(sorted({-(-nt * c // ATTN_CLASSES) for c in range(1, ATTN_CLASSES + 1)}))
    ao_p = pl.pallas_call(
        functools.partial(_attn_prompt_kernel, bounds=bounds),
        grid=(nb, nt),
        in_specs=[pair_tok(2 * LANES), pair_seq(2 * LANES), pair_seq(2 * V_DIM)],
        out_specs=pair_tok(2 * V_DIM),
        out_shape=sds((N_PAIRS, n_rows, 2 * V_DIM), BF16),
        compiler_params=_params(2),
        name="attn_prompt",
    )(q_p, k_p, v_p)

    seq3 = lambda v: v.reshape(nb, plen, v.shape[-1])
    rb = RET_SEQS if nb % RET_SEQS == 0 else 1
    tok = lambda w: pl.BlockSpec((rb, TILE, w), lambda b, t: (b, t, 0))
    ro_p, st_p = pl.pallas_call(
        _ret_prompt_kernel,
        grid=(nb // rb, nt),
        in_specs=[tok(RET_W), tok(RET_W), tok(RET_W), wspec],
        out_specs=[tok(RET_W), pl.BlockSpec((rb, RET_HEADS, RET_DK, RET_DV), lambda b, t: (b, 0, 0, 0))],
        out_shape=[sds((nb, plen, RET_W), F32), sds((nb, RET_HEADS, RET_DK, RET_DV), F32)],
        scratch_shapes=[pltpu.VMEM((rb, RET_HEADS, RET_DK, RET_DV), F32)],
        compiler_params=_params(2),
        name="ret_prompt",
    )(seq3(rq_p), seq3(rk_p), seq3(rv_p), gn)

    y_pad, cs_p = pl.pallas_call(
        functools.partial(_merge_ffn_prompt_kernel, nb=nb, nt=nt, n_x=mf_tm // TILE),
        grid=(n_rows // mf_tm,),
        in_specs=x_specs(mf_tm) + [wspec, pair_flat(mf_tm, 2 * V_DIM), flat(mf_tm, RET_W)] + [wspec] * 14,
        out_specs=[flat(mf_tm, D_MODEL), wspec],
        out_shape=[sds((n_rows, D_MODEL), F32), sds((nb, CONV_W - 1, D_FF), F32)],
        scratch_shapes=[pltpu.VMEM((8, D_FF), F32)],
        compiler_params=_params(1),
        name="merge_ffn_prompt",
    )(*([x_prompt] * (mf_tm // TILE)), meta_tile, ao_p, ro_p.reshape(n_rows, RET_W),
      g0, b0, w2, wao, wro, wo, g1, b1, wfi, cw, cb, wfo, g2, b2)
    y_p = y_pad.reshape(nb, plen, D_MODEL)[:, TILE:]
    c_p = seq3(c_p)
    kr_p = seq3(kr_p)

    xs = x_sample.reshape(db, D_MODEL)
    q_s, qabs_s, c_s, kr_s, rq_s, rk_s, rv_s = pl.pallas_call(
        _inproj_sample_kernel,
        in_specs=[wspec] * 9,
        out_specs=[wspec] * 7,
        out_shape=[sds((db, HEAD_W), BF16), sds((MLA_HEADS, db, KV_LORA), BF16), sds((db, KV_LORA), F32),
                   sds((db, ROPE_DIM), F32), sds((db, RET_W), F32), sds((db, RET_W), F32), sds((db, RET_W), F32)],
        compiler_params=pltpu.CompilerParams(vmem_limit_bytes=VMEM_LIMIT),
        name="inproj_sample",
    )(xs, tab_s, g0, b0, w1, qg, kg, wuq, wukt)

    qabs_t = jnp.transpose(qabs_s, (1, 0, 2))
    qr_s = q_s.reshape(db, MLA_HEADS, LANES)[:, :, NOPE_DIM:NOPE_DIM + ROPE_DIM]
    n_chunks = n_pages // PAGES_PER_CHUNK
    chunk_rows = PAGES_PER_CHUNK * PAGE_SIZE
    per_req = lambda s1, s2: pl.BlockSpec((1, s1, s2), lambda b, pt: (b, 0, 0))
    olat_s = pl.pallas_call(
        functools.partial(_attn_sample_kernel, n_chunks=n_chunks),
        grid_spec=pltpu.PrefetchScalarGridSpec(
            num_scalar_prefetch=1,
            grid=(db,),
            in_specs=[per_req(MLA_HEADS, KV_LORA), per_req(MLA_HEADS, ROPE_DIM), per_req(1, KV_LORA),
                      per_req(1, ROPE_DIM), _whole(pl.ANY), _whole(pl.ANY)],
            out_specs=per_req(MLA_HEADS, KV_LORA),
            scratch_shapes=[pltpu.VMEM((DMA_SLOTS, chunk_rows, KV_LORA), F32),
                            pltpu.VMEM((DMA_SLOTS, ROPE_DIM, chunk_rows), F32),
                            pltpu.VMEM((2, chunk_rows, KV_LORA), BF16),
                            pltpu.SemaphoreType.DMA((2, DMA_SLOTS))]),
        out_shape=sds((db, MLA_HEADS, KV_LORA), F32),
        compiler_params=_params(1),
        name="attn_sample",
    )(page_table, qabs_t, qr_s, c_s.reshape(db, 1, KV_LORA), kr_s.reshape(db, 1, ROPE_DIM),
      cache_kv_latent, jnp.swapaxes(cache_k_rope, 2, 3))

    rows = lambda w: pl.BlockSpec((RS_ROWS, w), lambda i: (i, 0))
    st_spec = pl.BlockSpec((RS_ROWS, RET_HEADS, RET_DK, RET_DV), lambda i: (i, 0, 0, 0))
    ro_s, st_s = pl.pallas_call(
        _ret_sample_kernel,
        grid=(db // RS_ROWS,),
        in_specs=[rows(RET_W), rows(RET_W), rows(RET_W), st_spec, wspec],
        out_specs=[rows(RET_W), st_spec],
        out_shape=[sds((db, RET_W), F32), sds((db, RET_HEADS, RET_DK, RET_DV), F32)],
        compiler_params=_params(1),
        name="ret_sample",
    )(rq_s, rk_s, rv_s, state_retention.reshape(db, RET_HEADS, RET_DK, RET_DV), gn)

    y_s, cs_s = pl.pallas_call(
        _merge_ffn_sample_kernel,
        in_specs=[wspec] * 19,
        out_specs=[wspec] * 2,
        out_shape=[sds((db, D_MODEL), F32), sds((db, (CONV_W - 1) * D_FF), F32)],
        compiler_params=pltpu.CompilerParams(vmem_limit_bytes=VMEM_LIMIT),
        name="merge_ffn_sample",
    )(xs, jnp.transpose(olat_s, (1, 0, 2)), ro_s, state_ffn_conv.reshape(db, (CONV_W - 1) * D_FF),
      g0, b0, w2, wuv3, wao, wro, wo, g1, b1, wfi, cw, cb, wfo, g2, b2)

    first = FRONT_PAD
    return (y_p,
            y_s.reshape(db, 1, D_MODEL),
            c_p[None, :, first:],
            kr_p[None, :, first:],
            st_p[None],
            cs_p[None],
            c_s.reshape(1, db, 1, KV_LORA),
            kr_s.reshape(1, db, 1, ROPE_DIM),
            st_s[None],
            cs_s.reshape(1, db, CONV_W - 1, D_FF))
```

```python
import functools

import numpy as np
import jax
import jax.numpy as jnp
from jax import lax
from jax.experimental import pallas as pl
from jax.experimental.pallas import tpu as pltpu

F32 = jnp.float32
BF16 = jnp.bfloat16

D_MODEL = 1024
N_META = 16
MLA_HEADS = 8
Q_LORA = 384
KV_LORA = 256
NOPE_DIM = 64
ROPE_DIM = 32
V_DIM = 64
ROPE_THETA = 10000.0
RET_HEADS = 4
RET_DK = 128
RET_DV = 128
D_FF = 2816
CONV_W = 3
PAGE_SIZE = 128
LN_EPS = 1e-5
RMS_EPS = 1e-6
DEPTH = 1
ALPHA = (2 * DEPTH) ** 0.25

LANES = 128
TILE = 128
FRONT_PAD = TILE - N_META
RET_W = RET_HEADS * RET_DK
HEAD_W = MLA_HEADS * LANES
N_PAIRS = MLA_HEADS // 2
ATTN_CLASSES = 6
N_TAB = 7
W1_COLS = Q_LORA + KV_LORA + 3 * RET_W + 2 * LANES
W2_COLS = RET_W + 2 * D_MODEL
NEG = -1e30
PAGES_PER_CHUNK = 32
IN_TM = 512
IN_SUB = 512
MF_TM = 512
MF_SUB = 256
RET_SEQS = 4
DMA_SLOTS = 4
VMEM_LIMIT = 56 * 1024 * 1024


def _dot(a, b):
    return jnp.dot(a, b, preferred_element_type=F32)


def _dot_nt(a, b):
    return lax.dot_general(a, b, (((1,), (1,)), ((), ())), preferred_element_type=F32)


def _layer_norm(x, g, b):
    mu = jnp.mean(x, -1, keepdims=True)
    xc = x - mu
    var = jnp.mean(xc * xc, -1, keepdims=True)
    return xc * lax.rsqrt(var + LN_EPS) * g + b


def _rms_norm(x, g):
    return x * lax.rsqrt(jnp.mean(x * x, -1, keepdims=True) + RMS_EPS) * g


def _group_norm(o):
    mu = jnp.mean(o, -1, keepdims=True)
    oc = o - mu
    var = jnp.mean(oc * oc, -1, keepdims=True)
    return oc * lax.rsqrt(var + LN_EPS)


def _log_gamma(h):
    return float(np.log(np.float32(1.0) - np.float32(2.0) ** np.float32(-5.0 - h), dtype=np.float32))


def _exp32(v):
    return float(np.exp(np.float32(v), dtype=np.float32))


def _whole(memory_space=pltpu.VMEM):
    return pl.BlockSpec(memory_space=memory_space)


def _params(n_axes):
    return pltpu.CompilerParams(dimension_semantics=("arbitrary",) * n_axes, vmem_limit_bytes=VMEM_LIMIT)


def _inproj_common(x, tab, g0, b0, w1, qg, kg, wuq):
    xn = _layer_norm(x, g0, b0)
    h = _dot(xn.astype(BF16), w1)
    o = 0
    q_lat = h[:, o:o + Q_LORA]; o += Q_LORA
    c_raw = h[:, o:o + KV_LORA]; o += KV_LORA
    rq = h[:, o:o + RET_W]; o += RET_W
    rk = h[:, o:o + RET_W]; o += RET_W
    rv = h[:, o:o + RET_W]; o += RET_W
    kr = h[:, o:o + LANES]; o += LANES
    krs = h[:, o:o + LANES]
    qc, qs1, qs2, rc, rs, kc, ks = [tab[:, i * LANES:(i + 1) * LANES] for i in range(N_TAB)]
    c = _rms_norm(c_raw, kg)
    krope = kr * kc + krs * ks
    q = _dot(_rms_norm(q_lat, qg).astype(BF16), wuq)
    q_heads = []
    for hh in range(MLA_HEADS):
        qh = q[:, hh * LANES:(hh + 1) * LANES]
        q_heads.append(qh * qc + pltpu.roll(qh, ROPE_DIM // 2, 1) * qs1
                       + pltpu.roll(qh, LANES - ROPE_DIM // 2, 1) * qs2)

    def rope_ret(x):
        return [x[:, hh * RET_DK:(hh + 1) * RET_DK] * rc
                + pltpu.roll(x[:, hh * RET_DK:(hh + 1) * RET_DK], RET_DK // 2, 1) * rs
                for hh in range(RET_HEADS)]

    rq_h = rope_ret(rq)
    rk_h = [v * (RET_DK ** -0.5) for v in rope_ret(rk)]
    return q_heads, c, krope, rq_h, rk_h, rv


def _prompt_rows(x_refs, meta_ref, tab_ref, first_piece, n_pieces, nt):
    row = lax.broadcasted_iota(jnp.int32, (TILE, 1), 0)
    xs, tabs, pos = [], [], []
    for j in range(first_piece, first_piece + n_pieces):
        t = (pl.program_id(0) * len(x_refs) + j) % nt
        xs.append(jnp.where(t == 0, meta_ref[...], x_refs[j][0]))
        p0 = pl.multiple_of(t * TILE, TILE)
        if tab_ref is not None:
            tabs.append(tab_ref[pl.ds(p0, TILE), :])
        pos.append(p0 + row)
    cat = lambda parts: parts[0] if len(parts) == 1 else jnp.concatenate(parts, 0)
    return cat(xs), (cat(tabs) if tabs else None), cat(pos)


def _inproj_prompt_kernel(*refs, nt, n_x):
    x_refs = refs[:n_x]
    (meta_ref, tab_ref, g0, b0, w1, qg, kg, wuq, wukp, e8, wuv,
     q_o, k_o, v_o, c_o, kr_o, rq_o, rk_o, rv_o) = refs[n_x:]
    per_group = min(IN_SUB // TILE, n_x)
    for s in range(n_x // per_group):
        rows = slice(s * per_group * TILE, (s + 1) * per_group * TILE)
        x, tab, pos = _prompt_rows(x_refs, meta_ref, tab_ref, s * per_group, per_group, nt)
        real = pos >= FRONT_PAD
        q_heads, c, krope, rq_h, rk_h, rv = _inproj_common(
            x, tab, g0[...], b0[...], w1[...], qg[...], kg[...], wuq[...])
        for hh in range(MLA_HEADS):
            q_o[hh // 2, rows, (hh % 2) * LANES:(hh % 2 + 1) * LANES] = q_heads[hh].astype(BF16)
        for hh in range(RET_HEADS):
            sl = slice(hh * RET_DK, (hh + 1) * RET_DK)
            rq_o[rows, sl] = rq_h[hh]
            rk_o[rows, sl] = jnp.where(real, rk_h[hh], 0.0)
        rv_o[rows, :] = rv
        cb = c.astype(BF16)
        k = (_dot(cb, wukp[...]) + _dot(krope.astype(BF16), e8[...])).astype(BF16)
        v = _dot(cb, wuv[...]).astype(BF16)
        for pr in range(N_PAIRS):
            k_o[pr, rows, :] = k[:, pr * 2 * LANES:(pr + 1) * 2 * LANES]
            v_o[pr, rows, :] = v[:, pr * 2 * V_DIM:(pr + 1) * 2 * V_DIM]
        c_o[rows, :] = c
        kr_o[rows, :] = krope[:, :ROPE_DIM]


def _inproj_sample_kernel(x_ref, tab_ref, g0, b0, w1, qg, kg, wuq, wukt,
                          q_o, qabs_o, c_o, kr_o, rq_o, rk_o, rv_o):
    q_heads, c, krope, rq_h, rk_h, rv = _inproj_common(
        x_ref[...], tab_ref[...], g0[...], b0[...], w1[...], qg[...], kg[...], wuq[...])
    for hh in range(MLA_HEADS):
        qb = q_heads[hh].astype(BF16)
        q_o[:, hh * LANES:(hh + 1) * LANES] = qb
        qabs_o[hh] = _dot(qb, wukt[hh]).astype(BF16)
    for hh in range(RET_HEADS):
        sl = slice(hh * RET_DK, (hh + 1) * RET_DK)
        rq_o[:, sl] = rq_h[hh]
        rk_o[:, sl] = rk_h[hh]
    rv_o[...] = rv
    c_o[...] = c
    kr_o[...] = krope[:, :ROPE_DIM]


def _attn_prompt_kernel(q_ref, k_ref, v_ref, o_ref, *, bounds):
    qi = pl.program_id(1)
    row = lax.broadcasted_iota(jnp.int32, (TILE, TILE), 0) + qi * TILE
    lane = lax.broadcasted_iota(jnp.int32, (TILE, TILE), 1)
    first = lane < V_DIM

    def run(lo, wt):
        width = wt * TILE

        def pair_body(pr, carry):
            q = q_ref[pr]
            k = k_ref[pr, :width, :]
            v = v_ref[pr, :width, :]
            outs = []
            for hh in range(2):
                s = _dot_nt(q[:, hh * LANES:(hh + 1) * LANES], k[:, hh * LANES:(hh + 1) * LANES])
                tiles = []
                for c in range(wt):
                    sc = s[:, c * TILE:(c + 1) * TILE]
                    if c == 0:
                        sc = jnp.where(lane >= FRONT_PAD, sc, NEG)
                    if c >= lo:
                        sc = jnp.where(lane + c * TILE <= row, sc, NEG)
                    tiles.append(sc)
                s = jnp.concatenate(tiles, 1)
                p = jnp.exp(s - jnp.max(s, -1, keepdims=True))
                inv_l = 1.0 / jnp.sum(p, -1, keepdims=True)
                outs.append(_dot(p.astype(BF16), v) * inv_l)
            o_ref[pr] = jnp.where(first, outs[0], outs[1]).astype(BF16)
            return carry

        lax.fori_loop(0, N_PAIRS, pair_body, 0, unroll=True)

    lo = 0
    for wt in bounds:
        pl.when((qi >= lo) & (qi < wt))(functools.partial(run, lo, wt))
        lo = wt


def _ret_prompt_kernel(rq_ref, rk_ref, rv_ref, gn_ref, o_ref, st_ref, state):
    t = pl.program_id(1)

    @pl.when(t == 0)
    def _():
        state[...] = jnp.zeros_like(state)

    ii = lax.broadcasted_iota(jnp.int32, (TILE, TILE), 0).astype(F32)
    jj = lax.broadcasted_iota(jnp.int32, (TILE, TILE), 1).astype(F32)
    diff = ii - jj
    idx = lax.broadcasted_iota(jnp.int32, (TILE, 1), 0).astype(F32)
    for hh in range(RET_HEADS):
        sl = slice(hh * RET_DK, (hh + 1) * RET_DK)
        lg = _log_gamma(hh)
        decay = jnp.where(diff >= 0, jnp.exp(jnp.maximum(diff, 0.0) * lg), 0.0)
        q_dec = jnp.exp((idx + 1.0) * lg)
        k_decay = jnp.exp((TILE - 1.0 - idx) * lg)
        for bb in range(rq_ref.shape[0]):
            q = rq_ref[bb, :, sl]
            k = rk_ref[bb, :, sl]
            vb = rv_ref[bb, :, sl].astype(BF16)
            st = state[bb, hh]
            inner = _dot_nt(q.astype(BF16), k.astype(BF16)) * decay
            o = _dot(inner.astype(BF16), vb) + _dot((q * q_dec).astype(BF16), st.astype(BF16))
            state[bb, hh] = _exp32(TILE * lg) * st + _dot((k * k_decay).T.astype(BF16), vb)
            o_ref[bb, :, sl] = _group_norm(o) * gn_ref[:, sl]

    @pl.when(t == pl.num_programs(1) - 1)
    def _():
        st_ref[...] = state[...]


def _merge(xn, attn_proj, ron, w2, wro, wo, g1, b1):
    g3 = _dot(xn.astype(BF16), w2)
    rg = g3[:, :RET_W]
    ga = g3[:, RET_W:RET_W + D_MODEL]
    gb = g3[:, RET_W + D_MODEL:]
    ret = (ron * (rg * jax.nn.sigmoid(rg))).astype(BF16)
    m = jax.nn.sigmoid(ga) * attn_proj + jax.nn.sigmoid(gb) * _dot(ret, wro)
    return _layer_norm(ALPHA * xn + _dot(m.astype(BF16), wo), g1, b1)


def _ffn_tail(x1, a, a1, a2, g, cw, cb, wfo, g2, b2):
    acc = cb + cw[0:1] * a2 + cw[1:2] * a1 + cw[2:3] * a
    hid = (jax.nn.gelu(acc) * g).astype(BF16)
    return _layer_norm(ALPHA * x1 + _dot(hid, wfo), g2, b2)


def _merge_ffn_prompt_kernel(*refs, nb, nt, n_x):
    x_refs = refs[:n_x]
    (meta_ref, ao_ref, ro_ref, g0, b0, w2, wao, wro, wo, g1, b1,
     wfi, cw, cb, wfo, g2, b2, y_hbm, cs_ref, carry, ybuf, ysem) = refs[n_x:]
    i = pl.program_id(0)
    last = pl.num_programs(0) - 1
    tm = n_x * TILE
    plen = nt * TILE

    def y_copies(step):
        out = []
        for j in range(n_x):
            g = step * n_x + j
            t = g % nt
            rows = pl.ds(pl.multiple_of(jnp.maximum(t - 1, 0) * TILE, TILE), TILE)
            out.append((t > 0, pltpu.make_async_copy(ybuf.at[step % 2, pl.ds(j * TILE, TILE)],
                                                     y_hbm.at[g // nt, rows], ysem.at[step % 2])))
        return out

    def y_wait(step):
        for is_prompt, cp in y_copies(step):
            pl.when(is_prompt)(cp.wait)

    @pl.when(i == 0)
    def _():
        carry[...] = jnp.zeros_like(carry)

    pl.when(i >= 2)(functools.partial(y_wait, i - 2))

    prev = carry[...]
    a_parts = []
    sub = min(MF_SUB, tm)
    for s in range(tm // sub):
        rows = slice(s * sub, (s + 1) * sub)
        x, _, pos = _prompt_rows(x_refs, meta_ref, None, s * (sub // TILE), sub // TILE, nt)
        xn = _layer_norm(x, g0[...], b0[...])
        attn_proj = None
        for pr in range(N_PAIRS):
            part = _dot(ao_ref[pr, rows, :], wao[pr * 2 * V_DIM:(pr + 1) * 2 * V_DIM, :])
            attn_proj = part if attn_proj is None else attn_proj + part
        x1 = _merge(xn, attn_proj, ro_ref[rows, :], w2[...], wro[...], wo[...], g1[...], b1[...])
        x1b = x1.astype(BF16)
        a = _dot(x1b, wfi[:, :D_FF])
        g = _dot(x1b, wfi[:, D_FF:])
        row = lax.broadcasted_iota(jnp.int32, (sub, 1), 0)
        a1 =jnp.where(row == 0, prev[7:8], pltpu.roll(a, 1, 0))
        a2 = jnp.where(row == 0, prev[6:7], jnp.where(row == 1, prev[7:8], pltpu.roll(a, 2, 0)))
        a1 = jnp.where(pos >= FRONT_PAD + 1, a1, 0.0)
        a2 = jnp.where(pos >= FRONT_PAD + 2, a2, 0.0)
        prev = a[sub - 8:]
        ybuf[i % 2, rows, :] = _ffn_tail(x1, a, a1, a2, g, cw[...], cb[...], wfo[...], g2[...], b2[...])
        a_parts.append(a)
    carry[...] = prev
    a = jnp.concatenate(a_parts, 0)
    for is_prompt, cp in y_copies(i):
        pl.when(is_prompt)(cp.start)

    @pl.when(i == last)
    def _():
        pl.when(i >= 1)(functools.partial(y_wait, i - 1))
        y_wait(i)

    for b in range(nb):
        end = (b + 1) * plen
        local = (end - 1) % tm + 1

        @pl.when(i == (end - 1) // tm)
        def _():
            cs_ref[b] = a[local - (CONV_W - 1):local]


def _merge_ffn_sample_kernel(x_ref, olat_ref, ro_ref, prev_ref, g0, b0, w2, wuv3, wao, wro, wo, g1, b1,
                             wfi, cw, cb, wfo, g2, b2, y_ref, cs_ref):
    xn = _layer_norm(x_ref[...], g0[...], b0[...])
    attn_proj = None
    for hh in range(MLA_HEADS):
        oh = _dot(olat_ref[hh].astype(BF16), wuv3[hh]).astype(BF16)
        part = _dot(oh, wao[hh * V_DIM:(hh + 1) * V_DIM, :])
        attn_proj = part if attn_proj is None else attn_proj + part
    x1 = _merge(xn, attn_proj, ro_ref[...], w2[...], wro[...], wo[...], g1[...], b1[...])
    x1b = x1.astype(BF16)
    a = _dot(x1b, wfi[:, :D_FF])
    g = _dot(x1b, wfi[:, D_FF:])
    prev0 = prev_ref[:, :D_FF]
    prev1 = prev_ref[:, D_FF:]
    y_ref[...] = _ffn_tail(x1, a, prev1, prev0, g, cw[...], cb[...], wfo[...], g2[...], b2[...])
    cs_ref[:, :D_FF] = prev1
    cs_ref[:, D_FF:] = a


def _attn_sample_kernel(pt_ref, qabs_ref, qr_ref, cnew_ref, krnew_ref, cache_c, cache_kr, o_ref,
                        cbuf, krbuf, cb16, sems, *, n_chunks):
    b = pl.program_id(0)
    nb = pl.num_programs(0)

    def copies(bb, ci, slot):
        out = []
        for p in range(PAGES_PER_CHUNK):
            page = pt_ref[bb, ci * PAGES_PER_CHUNK + p]
            rows = pl.ds(p * PAGE_SIZE, PAGE_SIZE)
            out.append((pltpu.make_async_copy(cache_c.at[0, page], cbuf.at[slot, rows], sems.at[0, slot]), p % 2))
            out.append((pltpu.make_async_copy(cache_kr.at[0, page], krbuf.at[slot, :, rows], sems.at[1, slot]),
                        (p + 1) % 2))
        return out

    def start(bb, ci, slot):
        for cp, prio in copies(bb, ci, slot):
            cp.start(priority=prio)

    ahead = DMA_SLOTS - 1
    ring = lambda g: lax.rem(g, DMA_SLOTS)

    @pl.when(b == 0)
    def _():
        for g0 in range(ahead):
            start(g0 // n_chunks, g0 % n_chunks, g0)

    qabs = qabs_ref[0]
    qr = qr_ref[0]
    m = jnp.full((MLA_HEADS, 1), NEG, F32)
    l = jnp.zeros((MLA_HEADS, 1), F32)
    acc = jnp.zeros((MLA_HEADS, KV_LORA), F32)
    pending = None
    for ci in range(n_chunks):
        g = b * n_chunks + ci
        slot = ring(g)
        half = ci % 2
        later, ci_ahead = divmod(ci + ahead, n_chunks)
        if later == 0:
            start(b, ci_ahead, ring(g + ahead))
        else:
            pl.when(b + later < nb)(functools.partial(start, b + later, ci_ahead, ring(g + ahead)))
        for cp, _ in copies(b, ci, slot):
            cp.wait()
        cb16[half] = cbuf[slot].astype(BF16)
        s = _dot_nt(qabs, cb16[half]) + _dot(qr, krbuf[slot].astype(BF16))
        if pending is not None:
            acc = pending[1] * acc + _dot(pending[0], cb16[pending[2]])
        m_new = jnp.maximum(m, jnp.max(s, -1, keepdims=True))
        p = jnp.exp(s - m_new)
        alpha = jnp.exp(m - m_new)
        l = alpha * l + jnp.sum(p, -1, keepdims=True)
        m = m_new
        pending = (p.astype(BF16), alpha, half)
    acc = pending[1] * acc + _dot(pending[0], cb16[pending[2]])
    cnew = cnew_ref[0]
    s_new = (jnp.sum(qabs.astype(F32) * cnew, -1, keepdims=True)
             + jnp.sum(qr.astype(F32) * krnew_ref[0], -1, keepdims=True))
    m_new = jnp.maximum(m, s_new)
    p_new = jnp.exp(s_new - m_new)
    alpha = jnp.exp(m - m_new)
    o_ref[0] = (alpha * acc + p_new * cnew) / (alpha * l + p_new)


RS_ROWS = 8


def _ret_sample_kernel(rq_ref, rk_ref, rv_ref, st_ref, gn_ref, o_ref, ns_ref):
    for hh in range(RET_HEADS):
        sl = slice(hh * RET_DK, (hh + 1) * RET_DK)
        gamma = _exp32(_log_gamma(hh))
        q8 = rq_ref[:, sl]
        k8 = rk_ref[:, sl]
        v8 = rv_ref[:, sl]
        qk = jnp.sum(q8 * k8, -1, keepdims=True)
        for r in range(RS_ROWS):
            kcol = jnp.broadcast_to(k8[r:r + 1], (RET_DK, RET_DK)).T
            qcol = jnp.broadcast_to(q8[r:r + 1], (RET_DK, RET_DK)).T
            st = st_ref[r, hh]
            v = v8[r:r + 1]
            ns_ref[r, hh] = gamma * st + kcol * v
            o = qk[r:r + 1] * v + jnp.sum(qcol * gamma * st, 0, keepdims=True)
            o_ref[r:r + 1, sl] = _group_norm(o) * gn_ref[:, sl]


def _rope_tables(pos):
    pos = np.asarray(pos, np.float32)
    n = pos.shape[0]

    def cos_sin(half):
        inv = np.float32(ROPE_THETA) ** (-np.arange(half, dtype=np.float32) / np.float32(half))
        ang = (pos[:, None] * inv[None, :]).astype(np.float32)
        return np.cos(ang), np.sin(ang)

    c16, s16 = cos_sin(ROPE_DIM // 2)
    c64, s64 = cos_sin(RET_DK // 2)
    z = lambda w: np.zeros((n, w), np.float32)
    scale = np.float32((NOPE_DIM + ROPE_DIM) ** -0.5)
    tail = LANES - NOPE_DIM - ROPE_DIM
    qc = np.concatenate([np.ones((n, NOPE_DIM), np.float32), c16, c16, z(tail)], 1) * scale
    qs1 = np.concatenate([z(NOPE_DIM + ROPE_DIM // 2), s16, z(tail)], 1) * scale
    qs2 = np.concatenate([z(NOPE_DIM), -s16, z(ROPE_DIM // 2 + tail)], 1) * scale
    rc = np.concatenate([c64, c64], 1)
    rs = np.concatenate([-s64, s64], 1)
    kc = np.concatenate([c16, c16, z(LANES - ROPE_DIM)], 1)
    ks = np.concatenate([-s16, s16, z(LANES - ROPE_DIM)], 1)
    return np.concatenate([qc, qs1, qs2, rc, rs, kc, ks], 1).astype(np.float32)


def _prep_weights(w_in, w_uq, w_uk, w_uv, w_ffn_in):
    sizes = (Q_LORA, KV_LORA, ROPE_DIM, RET_W, RET_W, RET_W, RET_W, D_MODEL, D_MODEL)
    offs = [0]
    for s in sizes:
        offs.append(offs[-1] + s)
    col = lambda i: w_in[:, offs[i]:offs[i + 1]]
    w_kr = col(2)
    half = ROPE_DIM // 2
    zpad = jnp.zeros((D_MODEL, LANES - ROPE_DIM), F32)
    w1 = jnp.concatenate([col(0), col(1), col(3), col(4), col(5), w_kr, zpad,
                          w_kr[:, half:], w_kr[:, :half], zpad], 1).astype(BF16)
    w2 = jnp.concatenate([col(6), col(7), col(8)], 1).astype(BF16)
    hd = NOPE_DIM + ROPE_DIM
    wuq = jnp.pad(w_uq, ((0, 0), (0, 0), (0, LANES - hd))).reshape(Q_LORA, HEAD_W).astype(BF16)
    wukp = jnp.pad(w_uk, ((0, 0), (0, 0), (0, LANES - NOPE_DIM))).reshape(KV_LORA, HEAD_W).astype(BF16)
    e1 = jnp.pad(jnp.eye(ROPE_DIM, dtype=F32), ((0, LANES - ROPE_DIM), (NOPE_DIM, LANES - hd)))
    e8 = jnp.tile(e1, (1, MLA_HEADS)).astype(BF16)
    wuv = w_uv.reshape(KV_LORA, MLA_HEADS * V_DIM).astype(BF16)
    wukt = jnp.pad(jnp.transpose(w_uk, (1, 2, 0)), ((0, 0), (0, LANES - NOPE_DIM), (0, 0))).astype(BF16)
    wuv3 = jnp.transpose(w_uv, (1, 0, 2)).astype(BF16)
    return w1, w2, wuq, wukp, e8, wuv, wukt, wuv3, w_ffn_in.astype(BF16)


def kernel(x_prompt, x_sample, cache_kv_latent, cache_k_rope, state_retention, state_ffn_conv, page_table, meta_tokens, ln0_g, ln0_b, w_in, q_norm_g, kv_norm_g, w_uq, w_uk, w_uv, ret_gn_g, w_attn_out, w_ret_out, w_o, ln1_g, ln1_b, w_ffn_in, conv_w, conv_b, w_ffn_out, ln2_g, ln2_b):
    assert w_in.shape[0] == DEPTH == 1
    nb, seq, _ = x_prompt.shape
    db, dec_seq, _ = x_sample.shape
    n_pages = page_table.shape[1]
    assert seq % TILE == 0 and dec_seq == 1 and db % RS_ROWS == 0
    assert n_pages % (2 * PAGES_PER_CHUNK) == 0 and db * (n_pages // PAGES_PER_CHUNK) >= DMA_SLOTS - 1
    nt = seq // TILE + 1
    plen = nt * TILE
    past = n_pages * PAGE_SIZE

    w1, w2, wuq, wukp, e8, wuv, wukt, wuv3, wfi = _prep_weights(w_in[0], w_uq[0], w_uk[0], w_uv[0], w_ffn_in[0])
    wao = w_attn_out[0].astype(BF16)
    wro = w_ret_out[0].astype(BF16)
    wo = w_o[0].astype(BF16)
    wfo = w_ffn_out[0].astype(BF16)
    row = lambda v: v.reshape(1, -1)
    g0, b0 = row(ln0_g), row(ln0_b)
    g1, b1, g2, b2 = row(ln1_g[0]), row(ln1_b[0]), row(ln2_g[0]), row(ln2_b[0])
    qg, kg, gn = row(q_norm_g[0]), row(kv_norm_g[0]), row(ret_gn_g[0])
    cw, cb = conv_w[0], row(conv_b[0])

    tab_p = jnp.asarray(_rope_tables(np.maximum(np.arange(plen) - FRONT_PAD, 0)))
    tab_s = jnp.broadcast_to(jnp.asarray(_rope_tables([past])), (db, N_TAB * LANES))
    n_rows = nb * plen
    meta_tile = jnp.concatenate([jnp.zeros((FRONT_PAD, D_MODEL), F32), meta_tokens.astype(F32)], 0)
    in_tm = IN_TM if n_rows % IN_TM == 0 else TILE
    mf_tm = MF_TM if n_rows % MF_TM == 0 else TILE

    def x_specs(tm):
        def spec(j):
            def index(i):
                g = i * (tm // TILE) + j
                return g // nt, jnp.maximum(g % nt - 1, 0), 0
            return pl.BlockSpec((1, TILE, D_MODEL), index)
        return [spec(j) for j in range(tm // TILE)]

    flat = lambda tm, w: pl.BlockSpec((tm, w), lambda i: (i, 0))
    pair_flat = lambda tm, w: pl.BlockSpec((N_PAIRS, tm, w), lambda i: (0, i, 0))
    pair_tok = lambda w: pl.BlockSpec((N_PAIRS, TILE, w), lambda b, t: (0, b * nt + t, 0))
    pair_seq = lambda w: pl.BlockSpec((N_PAIRS, plen, w), lambda b, t: (0, b, 0))
    wspec = _whole()
    sds = jax.ShapeDtypeStruct

    q_p, k_p, v_p, c_p, kr_p, rq_p, rk_p, rv_p = pl.pallas_call(
        functools.partial(_inproj_prompt_kernel, nt=nt, n_x=in_tm // TILE),
        grid=(n_rows // in_tm,),
        in_specs=x_specs(in_tm) + [wspec] * 11,
        out_specs=[pair_flat(in_tm, 2 * LANES), pair_flat(in_tm, 2 * LANES), pair_flat(in_tm, 2 * V_DIM),
                   flat(in_tm, KV_LORA), flat(in_tm, ROPE_DIM), flat(in_tm, RET_W), flat(in_tm, RET_W),
                   flat(in_tm, RET_W)],
        out_shape=[sds((N_PAIRS, n_rows, 2 * LANES), BF16), sds((N_PAIRS, n_rows, 2 * LANES), BF16),
                   sds((N_PAIRS, n_rows, 2 * V_DIM), BF16), sds((n_rows, KV_LORA), F32),
                   sds((n_rows, ROPE_DIM), F32), sds((n_rows, RET_W), F32),
                   sds((n_rows, RET_W), F32), sds((n_rows, RET_W), F32)],
        compiler_params=_params(1),
        name="inproj_prompt",
    )(*([x_prompt] * (in_tm // TILE)), meta_tile, tab_p, g0, b0, w1, qg, kg, wuq, wukp, e8, wuv)

    bounds = tuple(sorted({-(-nt * c // ATTN_CLASSES) for c in range(1, ATTN_CLASSES + 1)}))
    ao_p = pl.pallas_call(
        functools.partial(_attn_prompt_kernel, bounds=bounds),
        grid=(nb, nt),
        in_specs=[pair_tok(2 * LANES), pair_seq(2 * LANES), pair_seq(2 * V_DIM)],
        out_specs=pair_tok(2 * V_DIM),
        out_shape=sds((N_PAIRS, n_rows, 2 * V_DIM), BF16),
        compiler_params=_params(2),
        name="attn_prompt",
    )(q_p, k_p, v_p)

    seq3 = lambda v: v.reshape(nb, plen, v.shape[-1])
    rb = RET_SEQS if nb % RET_SEQS == 0 else 1
    tok = lambda w: pl.BlockSpec((rb, TILE, w), lambda b, t: (b, t, 0))
    ro_p, st_p = pl.pallas_call(
        _ret_prompt_kernel,
        grid=(nb // rb, nt),
        in_specs=[tok(RET_W), tok(RET_W), tok(RET_W), wspec],
        out_specs=[tok(RET_W), pl.BlockSpec((rb, RET_HEADS, RET_DK, RET_DV), lambda b, t: (b, 0, 0, 0))],
        out_shape=[sds((nb, plen, RET_W), F32), sds((nb, RET_HEADS, RET_DK, RET_DV), F32)],
        scratch_shapes=[pltpu.VMEM((rb, RET_HEADS, RET_DK, RET_DV), F32)],
        compiler_params=_params(2),
        name="ret_prompt",
    )(seq3(rq_p), seq3(rk_p), seq3(rv_p), gn)

    y_p, cs_p = pl.pallas_call(
        functools.partial(_merge_ffn_prompt_kernel, nb=nb, nt=nt, n_x=mf_tm // TILE),
        grid=(n_rows // mf_tm,),
        in_specs=x_specs(mf_tm) + [wspec, pair_flat(mf_tm, 2 * V_DIM), flat(mf_tm, RET_W)] + [wspec] * 14,
        out_specs=[_whole(pl.ANY), wspec],
        out_shape=[sds((nb, seq, D_MODEL), F32), sds((nb, CONV_W - 1, D_FF), F32)],
        scratch_shapes=[pltpu.VMEM((8, D_FF), F32), pltpu.VMEM((2, mf_tm, D_MODEL), F32),
                        pltpu.SemaphoreType.DMA((2,))],
        compiler_params=_params(1),
        name="merge_ffn_prompt",
    )(*([x_prompt] * (mf_tm // TILE)), meta_tile, ao_p, ro_p.reshape(n_rows, RET_W),
      g0, b0, w2, wao, wro, wo, g1, b1, wfi, cw, cb, wfo, g2, b2)
    c_p = seq3(c_p)
    kr_p = seq3(kr_p)

    xs = x_sample.reshape(db, D_MODEL)
    q_s, qabs_s, c_s, kr_s, rq_s, rk_s, rv_s = pl.pallas_call(
        _inproj_sample_kernel,
        in_specs=[wspec] * 9,
        out_specs=[wspec] * 7,
        out_shape=[sds((db, HEAD_W), BF16), sds((MLA_HEADS, db, KV_LORA), BF16), sds((db, KV_LORA), F32),
                   sds((db, ROPE_DIM), F32), sds((db, RET_W), F32), sds((db, RET_W), F32), sds((db, RET_W), F32)],
        compiler_params=pltpu.CompilerParams(vmem_limit_bytes=VMEM_LIMIT),
        name="inproj_sample",
    )(xs, tab_s, g0, b0, w1, qg, kg, wuq, wukt)

    qabs_t = jnp.transpose(qabs_s, (1, 0, 2))
    qr_s = q_s.reshape(db, MLA_HEADS, LANES)[:, :, NOPE_DIM:NOPE_DIM + ROPE_DIM]
    n_chunks = n_pages // PAGES_PER_CHUNK
    chunk_rows = PAGES_PER_CHUNK * PAGE_SIZE
    per_req = lambda s1, s2: pl.BlockSpec((1, s1, s2), lambda b, pt: (b, 0, 0))
    olat_s = pl.pallas_call(
        functools.partial(_attn_sample_kernel, n_chunks=n_chunks),
        grid_spec=pltpu.PrefetchScalarGridSpec(
            num_scalar_prefetch=1,
            grid=(db,),
            in_specs=[per_req(MLA_HEADS, KV_LORA), per_req(MLA_HEADS, ROPE_DIM), per_req(1, KV_LORA),
                      per_req(1, ROPE_DIM), _whole(pl.ANY), _whole(pl.ANY)],
            out_specs=per_req(MLA_HEADS, KV_LORA),
            scratch_shapes=[pltpu.VMEM((DMA_SLOTS, chunk_rows, KV_LORA), F32),
                            pltpu.VMEM((DMA_SLOTS, ROPE_DIM, chunk_rows), F32),
                            pltpu.VMEM((2, chunk_rows, KV_LORA), BF16),
                            pltpu.SemaphoreType.DMA((2, DMA_SLOTS))]),
        out_shape=sds((db, MLA_HEADS, KV_LORA), F32),
        compiler_params=_params(1),
        name="attn_sample",
    )(page_table, qabs_t, qr_s, c_s.reshape(db, 1, KV_LORA), kr_s.reshape(db, 1, ROPE_DIM),
      cache_kv_latent, jnp.swapaxes(cache_k_rope, 2, 3))

    rows = lambda w: pl.BlockSpec((RS_ROWS, w), lambda i: (i, 0))
    st_spec = pl.BlockSpec((RS_ROWS, RET_HEADS, RET_DK, RET_DV), lambda i: (i, 0, 0, 0))
    ro_s, st_s = pl.pallas_call(
        _ret_sample_kernel,
        grid=(db // RS_ROWS,),
        in_specs=[rows(RET_W), rows(RET_W), rows(RET_W), st_spec, wspec],
        out_specs=[rows(RET_W), st_spec],
        out_shape=[sds((db, RET_W), F32), sds((db, RET_HEADS, RET_DK, RET_DV), F32)],
        compiler_params=_params(1),
        name="ret_sample",
    )(rq_s, rk_s, rv_s, state_retention.reshape(db, RET_HEADS, RET_DK, RET_DV), gn)

    y_s, cs_s = pl.pallas_call(
        _merge_ffn_sample_kernel,
        in_specs=[wspec] * 19,
        out_specs=[wspec] * 2,
        out_shape=[sds((db, D_MODEL), F32), sds((db, (CONV_W - 1) * D_FF), F32)],
        compiler_params=pltpu.CompilerParams(vmem_limit_bytes=VMEM_LIMIT),
        name="merge_ffn_sample",
    )(xs, jnp.transpose(olat_s, (1, 0, 2)), ro_s, state_ffn_conv.reshape(db, (CONV_W - 1) * D_FF),
      g0, b0, w2, wuv3, wao, wro, wo, g1, b1, wfi, cw, cb, wfo, g2, b2)

    first = FRONT_PAD
    return (y_p,
            y_s.reshape(db, 1, D_MODEL),
            c_p[None, :, first:],
            kr_p[None, :, first:],
            st_p[None],
            cs_p[None],
            c_s.reshape(1, db, 1, KV_LORA),
            kr_s.reshape(1, db, 1, ROPE_DIM),
            st_s[None],
            cs_s.reshape(1, db, CONV_W - 1, D_FF))
```

```python
import functools

import numpy as np
import jax
import jax.numpy as jnp
from jax import lax
from jax.experimental import pallas as pl
from jax.experimental.pallas import tpu as pltpu

F32 = jnp.float32
BF16 = jnp.bfloat16

D_MODEL = 1024
N_META = 16
MLA_HEADS = 8
Q_LORA = 384
KV_LORA = 256
NOPE_DIM = 64
ROPE_DIM = 32
V_DIM = 64
ROPE_THETA = 10000.0
RET_HEADS = 4
RET_DK = 128
RET_DV = 128
D_FF = 2816
CONV_W = 3
PAGE_SIZE = 128
LN_EPS = 1e-5
RMS_EPS = 1e-6
DEPTH = 1
ALPHA = (2 * DEPTH) ** 0.25

LANES = 128
TILE = 128
FRONT_PAD = TILE - N_META
RET_W = RET_HEADS * RET_DK
HEAD_W = MLA_HEADS * LANES
N_PAIRS = MLA_HEADS // 2
ATTN_CLASSES = 6
N_TAB = 7
W1_COLS = Q_LORA + KV_LORA + 3 * RET_W + 2 * LANES
W2_COLS = RET_W + 2 * D_MODEL
NEG = -1e30
PAGES_PER_CHUNK = 32
IN_TM = 1024
IN_SUB = 512
MF_TM = 512
MF_SUB = 256
RET_SEQS = 4
DMA_SLOTS = 4
VMEM_LIMIT = 56 * 1024 * 1024


def _dot(a, b):
    return jnp.dot(a, b, preferred_element_type=F32)


def _dot_nt(a, b):
    return lax.dot_general(a, b, (((1,), (1,)), ((), ())), preferred_element_type=F32)


def _layer_norm(x, g, b):
    mu = jnp.mean(x, -1, keepdims=True)
    xc = x - mu
    var = jnp.mean(xc * xc, -1, keepdims=True)
    return xc * lax.rsqrt(var + LN_EPS) * g + b


def _rms_norm(x, g):
    return x * lax.rsqrt(jnp.mean(x * x, -1, keepdims=True) + RMS_EPS) * g


def _group_norm(o):
    mu = jnp.mean(o, -1, keepdims=True)
    oc = o - mu
    var = jnp.mean(oc * oc, -1, keepdims=True)
    return oc * lax.rsqrt(var + LN_EPS)


def _log_gamma(h):
    return float(np.log(np.float32(1.0) - np.float32(2.0) ** np.float32(-5.0 - h), dtype=np.float32))


def _exp32(v):
    return float(np.exp(np.float32(v), dtype=np.float32))


def _whole(memory_space=pltpu.VMEM):
    return pl.BlockSpec(memory_space=memory_space)


def _params(n_axes):
    return pltpu.CompilerParams(dimension_semantics=("arbitrary",) * n_axes, vmem_limit_bytes=VMEM_LIMIT)


def _inproj_common(x, tab, g0, b0, w1, qg, kg, wuq):
    xn = _layer_norm(x, g0, b0)
    h = _dot(xn.astype(BF16), w1)
    o = 0
    q_lat = h[:, o:o + Q_LORA]; o += Q_LORA
    c_raw = h[:, o:o + KV_LORA]; o += KV_LORA
    rq = h[:, o:o + RET_W]; o += RET_W
    rk = h[:, o:o + RET_W]; o += RET_W
    rv = h[:, o:o + RET_W]; o += RET_W
    kr = h[:, o:o + LANES]; o += LANES
    krs = h[:, o:o + LANES]
    qc, qs1, qs2, rc, rs, kc, ks = [tab[:, i * LANES:(i + 1) * LANES] for i in range(N_TAB)]
    c = _rms_norm(c_raw, kg)
    krope = kr * kc + krs * ks
    q = _dot(_rms_norm(q_lat, qg).astype(BF16), wuq)
    q_heads = []
    for hh in range(MLA_HEADS):
        qh = q[:, hh * LANES:(hh + 1) * LANES]
        q_heads.append(qh * qc + pltpu.roll(qh, ROPE_DIM // 2, 1) * qs1
                       + pltpu.roll(qh, LANES - ROPE_DIM // 2, 1) * qs2)

    def rope_ret(x):
        return [x[:, hh * RET_DK:(hh + 1) * RET_DK] * rc
                + pltpu.roll(x[:, hh * RET_DK:(hh + 1) * RET_DK], RET_DK // 2, 1) * rs
                for hh in range(RET_HEADS)]

    rq_h = rope_ret(rq)
    rk_h = [v * (RET_DK ** -0.5) for v in rope_ret(rk)]
    return q_heads, c, krope, rq_h, rk_h, rv


def _prompt_rows(x_refs, meta_ref, tab_ref, first_piece, n_pieces, nt):
    row = lax.broadcasted_iota(jnp.int32, (TILE, 1), 0)
    xs, tabs, pos = [], [], []
    for j in range(first_piece, first_piece + n_pieces):
        t = (pl.program_id(0) * len(x_refs) + j) % nt
        xs.append(jnp.where(t == 0, meta_ref[...], x_refs[j][0]))
        p0 = pl.multiple_of(t * TILE, TILE)
        if tab_ref is not None:
            tabs.append(tab_ref[pl.ds(p0, TILE), :])
        pos.append(p0 + row)
    cat = lambda parts: parts[0] if len(parts) == 1 else jnp.concatenate(parts, 0)
    return cat(xs), (cat(tabs) if tabs else None), cat(pos)


def _inproj_prompt_kernel(*refs, nt, n_x):
    x_refs = refs[:n_x]
    (meta_ref, tab_ref, g0, b0, w1, qg, kg, wuq, wukp, e8, wuv,
     q_o, k_o, v_o, c_o, kr_o, rq_o, rk_o, rv_o) = refs[n_x:]
    per_group = min(IN_SUB // TILE, n_x)
    for s in range(n_x // per_group):
        rows = slice(s * per_group * TILE, (s + 1) * per_group * TILE)
        x, tab, pos = _prompt_rows(x_refs, meta_ref, tab_ref, s * per_group, per_group, nt)
        real = pos >= FRONT_PAD
        q_heads, c, krope, rq_h, rk_h, rv = _inproj_common(
            x, tab, g0[...], b0[...], w1[...], qg[...], kg[...], wuq[...])
        for hh in range(MLA_HEADS):
            q_o[hh // 2, rows, (hh % 2) * LANES:(hh % 2 + 1) * LANES] = q_heads[hh].astype(BF16)
        for hh in range(RET_HEADS):
            sl = slice(hh * RET_DK, (hh + 1) * RET_DK)
            rq_o[rows, sl] = rq_h[hh]
            rk_o[rows, sl] = jnp.where(real, rk_h[hh], 0.0)
        rv_o[rows, :] = rv
        cb = c.astype(BF16)
        k = (_dot(cb, wukp[...]) + _dot(krope.astype(BF16), e8[...])).astype(BF16)
        v = _dot(cb, wuv[...]).astype(BF16)
        for pr in range(N_PAIRS):
            k_o[pr, rows, :] = k[:, pr * 2 * LANES:(pr + 1) * 2 * LANES]
            v_o[pr, rows, :] = v[:, pr * 2 * V_DIM:(pr + 1) * 2 * V_DIM]
        c_o[rows, :] = c
        kr_o[rows, :] = krope[:, :ROPE_DIM]


def _inproj_sample_kernel(x_ref, tab_ref, g0, b0, w1, qg, kg, wuq, wukt,
                          q_o, qabs_o, c_o, kr_o, rq_o, rk_o, rv_o):
    q_heads, c, krope, rq_h, rk_h, rv = _inproj_common(
        x_ref[...], tab_ref[...], g0[...], b0[...], w1[...], qg[...], kg[...], wuq[...])
    for hh in range(MLA_HEADS):
        qb = q_heads[hh].astype(BF16)
        q_o[:, hh * LANES:(hh + 1) * LANES] = qb
        qabs_o[hh] = _dot(qb, wukt[hh]).astype(BF16)
    for hh in range(RET_HEADS):
        sl = slice(hh * RET_DK, (hh + 1) * RET_DK)
        rq_o[:, sl] = rq_h[hh]
        rk_o[:, sl] = rk_h[hh]
    rv_o[...] = rv
    c_o[...] = c
    kr_o[...] = krope[:, :ROPE_DIM]


def _attn_prompt_kernel(q_ref, k_ref, v_ref, o_ref, *, bounds):
    qi = pl.program_id(1)
    row = lax.broadcasted_iota(jnp.int32, (TILE, TILE), 0) + qi * TILE
    lane = lax.broadcasted_iota(jnp.int32, (TILE, TILE), 1)
    first = lane < V_DIM

    def run(lo, wt):
        width = wt * TILE

        def pair_body(pr, carry):
            q = q_ref[pr]
            k = k_ref[pr, :width, :]
            v = v_ref[pr, :width, :]
            probs, inv_l = [], []
            for hh in range(2):
                s = _dot_nt(q[:, hh * LANES:(hh + 1) * LANES], k[:, hh * LANES:(hh + 1) * LANES])
                tiles = []
                for c in range(wt):
                    sc = s[:, c * TILE:(c + 1) * TILE]
                    if c == 0:
                        sc = jnp.where(lane >= FRONT_PAD, sc, NEG)
                    if c >= lo:
                        sc = jnp.where(lane + c * TILE <= row, sc, NEG)
                    tiles.append(sc)
                s = jnp.concatenate(tiles, 1)
                p = jnp.exp(s - jnp.max(s, -1, keepdims=True))
                inv_l.append(1.0 / jnp.sum(p, -1, keepdims=True))
                probs.append(p.astype(BF16))
            pv = _dot(jnp.concatenate(probs, 0), v)
            o_ref[pr] = jnp.where(first, pv[:TILE] * inv_l[0], pv[TILE:] * inv_l[1]).astype(BF16)
            return carry

        lax.fori_loop(0, N_PAIRS, pair_body, 0, unroll=True)

    lo = 0
    for wt in bounds:
        pl.when((qi >= lo) & (qi < wt))(functools.partial(run, lo, wt))
        lo = wt


def _ret_prompt_kernel(rq_ref, rk_ref, rv_ref, gn_ref, o_ref, st_ref, state):
    t = pl.program_id(1)

    @pl.when(t == 0)
    def _():
        state[...] = jnp.zeros_like(state)

    ii = lax.broadcasted_iota(jnp.int32, (TILE, TILE), 0).astype(F32)
    jj = lax.broadcasted_iota(jnp.int32, (TILE, TILE), 1).astype(F32)
    diff = ii - jj
    idx = lax.broadcasted_iota(jnp.int32, (TILE, 1), 0).astype(F32)
    pw = 2 * RET_DK
    left = lax.broadcasted_iota(jnp.int32, (TILE, pw), 1) < RET_DK
    top = lax.broadcasted_iota(jnp.int32, (pw, 1), 0) < RET_DK
    same_head = top == (lax.broadcasted_iota(jnp.int32, (pw, pw), 1) < RET_DV)

    def block_diag(x):
        return jnp.concatenate([jnp.where(left, x, 0.0), jnp.where(left, 0.0, x)], 0).astype(BF16)

    for pp in range(RET_HEADS // 2):
        sl = slice(pp * pw, (pp + 1) * pw)
        lg = (_log_gamma(2 * pp), _log_gamma(2 * pp + 1))
        decay = jnp.concatenate([jnp.where(diff >= 0, jnp.exp(jnp.maximum(diff, 0.0) * g), 0.0) for g in lg], 1)
        q_dec = jnp.where(left, jnp.exp((idx + 1.0) * lg[0]), jnp.exp((idx + 1.0) * lg[1]))
        k_decay = jnp.where(left, jnp.exp((TILE - 1.0 - idx) * lg[0]), jnp.exp((TILE - 1.0 - idx) * lg[1]))
        chunk_decay = jnp.where(top, _exp32(TILE * lg[0]), _exp32(TILE * lg[1]))
        for bb in range(rq_ref.shape[0]):
            q = rq_ref[bb, :, sl]
            k = rk_ref[bb, :, sl]
            v = rv_ref[bb, :, sl]
            st = state[bb, pp]
            inner = _dot_nt(q.astype(BF16), block_diag(k)) * decay
            o = _dot(inner.astype(BF16), block_diag(v)) + _dot((q * q_dec).astype(BF16), st.astype(BF16))
            kv = _dot((k * k_decay).T.astype(BF16), v.astype(BF16))
            state[bb, pp] = chunk_decay * st + jnp.where(same_head, kv, 0.0)
            for h2 in range(2):
                hs = slice(pp * pw + h2 * RET_DV, pp * pw + (h2 + 1) * RET_DV)
                o_ref[bb, :, hs] = _group_norm(o[:, h2 * RET_DV:(h2 + 1) * RET_DV]) * gn_ref[:, hs]

    @pl.when(t == pl.num_programs(1) - 1)
    def _():
        for pp in range(RET_HEADS // 2):
            for h2 in range(2):
                st_ref[:, 2 * pp + h2] = state[:, pp, h2 * RET_DK:(h2 + 1) * RET_DK, h2 * RET_DV:(h2 + 1) * RET_DV]


def _merge(xn, attn_proj, ron, w2, wro, wo, g1, b1):
    g3 = _dot(xn.astype(BF16), w2)
    rg = g3[:, :RET_W]
    ga = g3[:, RET_W:RET_W + D_MODEL]
    gb = g3[:, RET_W + D_MODEL:]
    ret = (ron * (rg * jax.nn.sigmoid(rg))).astype(BF16)
    m = jax.nn.sigmoid(ga) * attn_proj + jax.nn.sigmoid(gb) * _dot(ret, wro)
    return _layer_norm(ALPHA * xn + _dot(m.astype(BF16), wo), g1, b1)


def _ffn_tail(x1, a, a1, a2, g, cw, cb, wfo, g2, b2):
    acc = cb + cw[0:1] * a2 + cw[1:2] * a1 + cw[2:3] * a
    hid = (jax.nn.gelu(acc) * g).astype(BF16)
    return _layer_norm(ALPHA * x1 + _dot(hid, wfo), g2, b2)


def _merge_ffn_prompt_kernel(*refs, nb, nt, n_x):
    x_refs = refs[:n_x]
    (meta_ref, ao_ref, ro_ref, g0, b0, w2, wao, wro, wo, g1, b1,
     wfi, cw, cb, wfo, g2, b2, y_hbm, cs_ref, carry, ybuf, ysem) = refs[n_x:]
    i = pl.program_id(0)
    last = pl.num_programs(0) - 1
    tm = n_x * TILE
    plen = nt * TILE

    def y_copies(step):
        out = []
        for j in range(n_x):
            g = step * n_x + j
            t = g % nt
            rows = pl.ds(pl.multiple_of(jnp.maximum(t - 1, 0) * TILE, TILE), TILE)
            out.append((t > 0, pltpu.make_async_copy(ybuf.at[step % 2, pl.ds(j * TILE, TILE)],
                                                     y_hbm.at[g // nt, rows], ysem.at[step % 2])))
        return out

    def y_wait(step):
        for is_prompt, cp in y_copies(step):
            pl.when(is_prompt)(cp.wait)

    @pl.when(i == 0)
    def _():
        carry[...] = jnp.zeros_like(carry)

    pl.when(i >= 2)(functools.partial(y_wait, i - 2))

    prev = carry[...]
    a_parts = []
    sub = min(MF_SUB, tm)
    for s in range(tm // sub):
        rows = slice(s * sub, (s + 1) * sub)
        x, _, pos = _prompt_rows(x_refs, meta_ref, None, s * (sub // TILE), sub // TILE, nt)
        xn = _layer_norm(x, g0[...], b0[...])
        ao = jnp.concatenate([ao_ref[pr, rows, :] for pr in range(N_PAIRS)], 1)
        x1 = _merge(xn, _dot(ao, wao[...]), ro_ref[rows, :], w2[...], wro[...], wo[...], g1[...], b1[...])
        x1b = x1.astype(BF16)
        a = _dot(x1b, wfi[:, :D_FF])
        g = _dot(x1b, wfi[:, D_FF:])
        row = lax.broadcasted_iota(jnp.int32, (sub, 1), 0)
        a1 =jnp.where(row == 0, prev[7:8], pltpu.roll(a, 1, 0))
        a2 = jnp.where(row == 0, prev[6:7], jnp.where(row == 1, prev[7:8], pltpu.roll(a, 2, 0)))
        a1 = jnp.where(pos >= FRONT_PAD + 1, a1, 0.0)
        a2 = jnp.where(pos >= FRONT_PAD + 2, a2, 0.0)
        prev = a[sub - 8:]
        ybuf[i % 2, rows, :] = _ffn_tail(x1, a, a1, a2, g, cw[...], cb[...], wfo[...], g2[...], b2[...])
        a_parts.append(a)
    carry[...] = prev
    a = jnp.concatenate(a_parts, 0)
    for is_prompt, cp in y_copies(i):
        pl.when(is_prompt)(cp.start)

    @pl.when(i == last)
    def _():
        pl.when(i >= 1)(functools.partial(y_wait, i - 1))
        y_wait(i)

    for b in range(nb):
        end = (b + 1) * plen
        local = (end - 1) % tm + 1

        @pl.when(i == (end - 1) // tm)
        def _():
            cs_ref[b] = a[local - (CONV_W - 1):local]


def _merge_ffn_sample_kernel(x_ref, olat_ref, ro_ref, prev_ref, g0, b0, w2, wuv3, wao, wro, wo, g1, b1,
                             wfi, cw, cb, wfo, g2, b2, y_ref, cs_ref):
    xn = _layer_norm(x_ref[...], g0[...], b0[...])
    attn_proj = None
    for hh in range(MLA_HEADS):
        oh = _dot(olat_ref[hh].astype(BF16), wuv3[hh]).astype(BF16)
        part = _dot(oh, wao[hh * V_DIM:(hh + 1) * V_DIM, :])
        attn_proj = part if attn_proj is None else attn_proj + part
    x1 = _merge(xn, attn_proj, ro_ref[...], w2[...], wro[...], wo[...], g1[...], b1[...])
    x1b = x1.astype(BF16)
    a = _dot(x1b, wfi[:, :D_FF])
    g = _dot(x1b, wfi[:, D_FF:])
    prev0 = prev_ref[:, :D_FF]
    prev1 = prev_ref[:, D_FF:]
    y_ref[...] = _ffn_tail(x1, a, prev1, prev0, g, cw[...], cb[...], wfo[...], g2[...], b2[...])
    cs_ref[:, :D_FF] = prev1
    cs_ref[:, D_FF:] = a


def _attn_sample_kernel(pt_ref, qabs_ref, qr_ref, cnew_ref, krnew_ref, cache_c, cache_kr, o_ref,
                        cbuf, krbuf, cb16, sems, *, n_chunks):
    b = pl.program_id(0)
    nb = pl.num_programs(0)

    def copies(bb, ci, slot):
        out = []
        for p in range(PAGES_PER_CHUNK):
            page = pt_ref[bb, ci * PAGES_PER_CHUNK + p]
            rows = pl.ds(p * PAGE_SIZE, PAGE_SIZE)
            out.append((pltpu.make_async_copy(cache_c.at[0, page], cbuf.at[slot, rows], sems.at[0, slot]), p % 2))
            out.append((pltpu.make_async_copy(cache_kr.at[0, page], krbuf.at[slot, :, rows], sems.at[1, slot]),
                        (p + 1) % 2))
        return out

    def start(bb, ci, slot):
        for cp, prio in copies(bb, ci, slot):
            cp.start(priority=prio)

    ahead = DMA_SLOTS - 1
    ring = lambda g: lax.rem(g, DMA_SLOTS)

    @pl.when(b == 0)
    def _():
        for g0 in range(ahead):
            start(g0 // n_chunks, g0 % n_chunks, g0)

    qabs = qabs_ref[0]
    qr = qr_ref[0]
    m = jnp.full((MLA_HEADS, 1), NEG, F32)
    l = jnp.zeros((MLA_HEADS, 1), F32)
    acc = jnp.zeros((MLA_HEADS, KV_LORA), F32)
    pending = None
    for ci in range(n_chunks):
        g = b * n_chunks + ci
        slot = ring(g)
        half = ci % 2
        later, ci_ahead = divmod(ci + ahead, n_chunks)
        if later == 0:
            start(b, ci_ahead, ring(g + ahead))
        else:
            pl.when(b + later < nb)(functools.partial(start, b + later, ci_ahead, ring(g + ahead)))
        for cp, _ in copies(b, ci, slot):
            cp.wait()
        cb16[half] = cbuf[slot].astype(BF16)
        s = _dot_nt(qabs, cb16[half]) + _dot(qr, krbuf[slot].astype(BF16))
        if pending is not None:
            acc = pending[1] * acc + _dot(pending[0], cb16[pending[2]])
        m_new = jnp.maximum(m, jnp.max(s, -1, keepdims=True))
        p = jnp.exp(s - m_new)
        alpha = jnp.exp(m - m_new)
        l = alpha * l + jnp.sum(p, -1, keepdims=True)
        m = m_new
        pending = (p.astype(BF16), alpha, half)
    acc = pending[1] * acc + _dot(pending[0], cb16[pending[2]])
    cnew = cnew_ref[0]
    s_new = (jnp.sum(qabs.astype(F32) * cnew, -1, keepdims=True)
             + jnp.sum(qr.astype(F32) * krnew_ref[0], -1, keepdims=True))
    m_new = jnp.maximum(m, s_new)
    p_new = jnp.exp(s_new - m_new)
    alpha = jnp.exp(m - m_new)
    o_ref[0] = (alpha * acc + p_new * cnew) / (alpha * l + p_new)


RS_ROWS = 8


def _ret_sample_kernel(rq_ref, rk_ref, rv_ref, st_ref, gn_ref, o_ref, ns_ref):
    for hh in range(RET_HEADS):
        sl = slice(hh * RET_DK, (hh + 1) * RET_DK)
        gamma = _exp32(_log_gamma(hh))
        q8 = rq_ref[:, sl]
        k8 = rk_ref[:, sl]
        v8 = rv_ref[:, sl]
        qk = jnp.sum(q8 * k8, -1, keepdims=True)
        for r in range(RS_ROWS):
            kcol = jnp.broadcast_to(k8[r:r + 1], (RET_DK, RET_DK)).T
            qcol = jnp.broadcast_to(q8[r:r + 1], (RET_DK, RET_DK)).T
            st = st_ref[r, hh]
            v = v8[r:r + 1]
            ns_ref[r, hh] = gamma * st + kcol * v
            o = qk[r:r + 1] * v + jnp.sum(qcol * gamma * st, 0, keepdims=True)
            o_ref[r:r + 1, sl] = _group_norm(o) * gn_ref[:, sl]


def _rope_tables(pos):
    pos = np.asarray(pos, np.float64)
    n = pos.shape[0]

    def cos_sin(half):
        inv = np.float64(ROPE_THETA) ** (-np.arange(half, dtype=np.float64) / half)
        ang = pos[:, None] * inv[None, :]
        return np.cos(ang), np.sin(ang)

    c16, s16 = cos_sin(ROPE_DIM // 2)
    c64, s64 = cos_sin(RET_DK // 2)
    z = lambda w: np.zeros((n, w), np.float64)
    scale = (NOPE_DIM + ROPE_DIM) ** -0.5
    tail = LANES - NOPE_DIM - ROPE_DIM
    qc = np.concatenate([np.ones((n, NOPE_DIM), np.float64), c16, c16, z(tail)], 1) * scale
    qs1 = np.concatenate([z(NOPE_DIM + ROPE_DIM // 2), s16, z(tail)], 1) * scale
    qs2 = np.concatenate([z(NOPE_DIM), -s16, z(ROPE_DIM // 2 + tail)], 1) * scale
    rc = np.concatenate([c64, c64], 1)
    rs = np.concatenate([-s64, s64], 1)
    kc = np.concatenate([c16, c16, z(LANES - ROPE_DIM)], 1)
    ks = np.concatenate([-s16, s16, z(LANES - ROPE_DIM)], 1)
    return np.concatenate([qc, qs1, qs2, rc, rs, kc, ks], 1).astype(np.float32)


def _prep_weights(w_in, w_uq, w_uk, w_uv, w_ffn_in):
    sizes = (Q_LORA, KV_LORA, ROPE_DIM, RET_W, RET_W, RET_W, RET_W, D_MODEL, D_MODEL)
    offs = [0]
    for s in sizes:
        offs.append(offs[-1] + s)
    col = lambda i: w_in[:, offs[i]:offs[i + 1]]
    w_kr = col(2)
    half = ROPE_DIM // 2
    zpad = jnp.zeros((D_MODEL, LANES - ROPE_DIM), F32)
    w1 = jnp.concatenate([col(0), col(1), col(3), col(4), col(5), w_kr, zpad,
                          w_kr[:, half:], w_kr[:, :half], zpad], 1).astype(BF16)
    w2 = jnp.concatenate([col(6), col(7), col(8)], 1).astype(BF16)
    hd = NOPE_DIM + ROPE_DIM
    wuq = jnp.pad(w_uq, ((0, 0), (0, 0), (0, LANES - hd))).reshape(Q_LORA, HEAD_W).astype(BF16)
    wukp = jnp.pad(w_uk, ((0, 0), (0, 0), (0, LANES - NOPE_DIM))).reshape(KV_LORA, HEAD_W).astype(BF16)
    e1 = jnp.pad(jnp.eye(ROPE_DIM, dtype=F32), ((0, LANES - ROPE_DIM), (NOPE_DIM, LANES - hd)))
    e8 = jnp.tile(e1, (1, MLA_HEADS)).astype(BF16)
    wuv = w_uv.reshape(KV_LORA, MLA_HEADS * V_DIM).astype(BF16)
    wukt = jnp.pad(jnp.transpose(w_uk, (1, 2, 0)), ((0, 0), (0, LANES - NOPE_DIM), (0, 0))).astype(BF16)
    wuv3 = jnp.transpose(w_uv, (1, 0, 2)).astype(BF16)
    return w1, w2, wuq, wukp, e8, wuv, wukt, wuv3, w_ffn_in.astype(BF16)


def kernel(x_prompt, x_sample, cache_kv_latent, cache_k_rope, state_retention, state_ffn_conv, page_table, meta_tokens, ln0_g, ln0_b, w_in, q_norm_g, kv_norm_g, w_uq, w_uk, w_uv, ret_gn_g, w_attn_out, w_ret_out, w_o, ln1_g, ln1_b, w_ffn_in, conv_w, conv_b, w_ffn_out, ln2_g, ln2_b):
    assert w_in.shape[0] == DEPTH == 1
    nb, seq, _ = x_prompt.shape
    db, dec_seq, _ = x_sample.shape
    n_pages = page_table.shape[1]
    assert seq % TILE == 0 and dec_seq == 1 and db % RS_ROWS == 0
    assert n_pages % (2 * PAGES_PER_CHUNK) == 0 and db * (n_pages // PAGES_PER_CHUNK) >= DMA_SLOTS - 1
    nt = seq // TILE + 1
    plen = nt * TILE
    past = n_pages * PAGE_SIZE

    w1, w2, wuq, wukp, e8, wuv, wukt, wuv3, wfi = _prep_weights(w_in[0], w_uq[0], w_uk[0], w_uv[0], w_ffn_in[0])
    wao = w_attn_out[0].astype(BF16)
    wro = w_ret_out[0].astype(BF16)
    wo = w_o[0].astype(BF16)
    wfo = w_ffn_out[0].astype(BF16)
    row = lambda v: v.reshape(1, -1)
    g0, b0 = row(ln0_g), row(ln0_b)
    g1, b1, g2, b2 = row(ln1_g[0]), row(ln1_b[0]), row(ln2_g[0]), row(ln2_b[0])
    qg, kg, gn = row(q_norm_g[0]), row(kv_norm_g[0]), row(ret_gn_g[0])
    cw, cb = conv_w[0], row(conv_b[0])

    tab_p = jnp.asarray(_rope_tables(np.maximum(np.arange(plen) - FRONT_PAD, 0)))
    tab_s = jnp.broadcast_to(jnp.asarray(_rope_tables([past])), (db, N_TAB * LANES))
    n_rows = nb * plen
    meta_tile = jnp.concatenate([jnp.zeros((FRONT_PAD, D_MODEL), F32), meta_tokens.astype(F32)], 0)
    in_tm = IN_TM if n_rows % IN_TM == 0 else TILE
    mf_tm = MF_TM if n_rows % MF_TM == 0 else TILE

    def x_specs(tm):
        def spec(j):
            def index(i):
                g = i * (tm // TILE) + j
                return g // nt, jnp.maximum(g % nt - 1, 0), 0
            return pl.BlockSpec((1, TILE, D_MODEL), index)
        return [spec(j) for j in range(tm // TILE)]

    flat = lambda tm, w: pl.BlockSpec((tm, w), lambda i: (i, 0))
    pair_flat = lambda tm, w: pl.BlockSpec((N_PAIRS, tm, w), lambda i: (0, i, 0))
    pair_tok = lambda w: pl.BlockSpec((N_PAIRS, TILE, w), lambda b, t: (0, b * nt + t, 0))
    pair_seq = lambda w: pl.BlockSpec((N_PAIRS, plen, w), lambda b, t: (0, b, 0))
    wspec = _whole()
    sds = jax.ShapeDtypeStruct

    q_p, k_p, v_p, c_p, kr_p, rq_p, rk_p, rv_p = pl.pallas_call(
        functools.partial(_inproj_prompt_kernel, nt=nt, n_x=in_tm // TILE),
        grid=(n_rows // in_tm,),
        in_specs=x_specs(in_tm) + [wspec] * 11,
        out_specs=[pair_flat(in_tm, 2 * LANES), pair_flat(in_tm, 2 * LANES), pair_flat(in_tm, 2 * V_DIM),
                   flat(in_tm, KV_LORA), flat(in_tm, ROPE_DIM), flat(in_tm, RET_W), flat(in_tm, RET_W),
                   flat(in_tm, RET_W)],
        out_shape=[sds((N_PAIRS, n_rows, 2 * LANES), BF16), sds((N_PAIRS, n_rows, 2 * LANES), BF16),
                   sds((N_PAIRS, n_rows, 2 * V_DIM), BF16), sds((n_rows, KV_LORA), F32),
                   sds((n_rows, ROPE_DIM), F32), sds((n_rows, RET_W), F32),
                   sds((n_rows, RET_W), F32), sds((n_rows, RET_W), F32)],
        compiler_params=_params(1),
        name="inproj_prompt",
    )(*([x_prompt] * (in_tm // TILE)), meta_tile, tab_p, g0, b0, w1, qg, kg, wuq, wukp, e8, wuv)

    bounds = tuple(sorted({-(-nt * c // ATTN_CLASSES) for c in range(1, ATTN_CLASSES + 1)}))
    ao_p = pl.pallas_call(
        functools.partial(_attn_prompt_kernel, bounds=bounds),
        grid=(nb, nt),
        in_specs=[pair_tok(2 * LANES), pair_seq(2 * LANES), pair_seq(2 * V_DIM)],
        out_specs=pair_tok(2 * V_DIM),
        out_shape=sds((N_PAIRS, n_rows, 2 * V_DIM), BF16),
        compiler_params=_params(2),
        name="attn_prompt",
    )(q_p, k_p, v_p)

    seq3 = lambda v: v.reshape(nb, plen, v.shape[-1])
    rb = RET_SEQS if nb % RET_SEQS == 0 else 1
    tok = lambda w: pl.BlockSpec((rb, TILE, w), lambda b, t: (b, t, 0))
    ro_p, st_p = pl.pallas_call(
        _ret_prompt_kernel,
        grid=(nb // rb, nt),
        in_specs=[tok(RET_W), tok(RET_W), tok(RET_W), wspec],
        out_specs=[tok(RET_W), pl.BlockSpec((rb, RET_HEADS, RET_DK, RET_DV), lambda b, t: (b, 0, 0, 0))],
        out_shape=[sds((nb, plen, RET_W), F32), sds((nb, RET_HEADS, RET_DK, RET_DV), F32)],
        scratch_shapes=[pltpu.VMEM((rb, RET_HEADS // 2, 2 * RET_DK, 2 * RET_DV), F32)],
        compiler_params=_params(2),
        name="ret_prompt",
    )(seq3(rq_p), seq3(rk_p), seq3(rv_p), gn)

    y_p, cs_p = pl.pallas_call(
        functools.partial(_merge_ffn_prompt_kernel, nb=nb, nt=nt, n_x=mf_tm // TILE),
        grid=(n_rows // mf_tm,),
        in_specs=x_specs(mf_tm) + [wspec, pair_flat(mf_tm, 2 * V_DIM), flat(mf_tm, RET_W)] + [wspec] * 14,
        out_specs=[_whole(pl.ANY), wspec],
        out_shape=[sds((nb, seq, D_MODEL), F32), sds((nb, CONV_W - 1, D_FF), F32)],
        scratch_shapes=[pltpu.VMEM((8, D_FF), F32), pltpu.VMEM((2, mf_tm, D_MODEL), F32),
                        pltpu.SemaphoreType.DMA((2,))],
        compiler_params=_params(1),
        name="merge_ffn_prompt",
    )(*([x_prompt] * (mf_tm // TILE)), meta_tile, ao_p, ro_p.reshape(n_rows, RET_W),
      g0, b0, w2, wao, wro, wo, g1, b1, wfi, cw, cb, wfo, g2, b2)
    c_p = seq3(c_p)
    kr_p = seq3(kr_p)

    xs = x_sample.reshape(db, D_MODEL)
    q_s, qabs_s, c_s, kr_s, rq_s, rk_s, rv_s = pl.pallas_call(
        _inproj_sample_kernel,
        in_specs=[wspec] * 9,
        out_specs=[wspec] * 7,
        out_shape=[sds((db, HEAD_W), BF16), sds((MLA_HEADS, db, KV_LORA), BF16), sds((db, KV_LORA), F32),
                   sds((db, ROPE_DIM), F32), sds((db, RET_W), F32), sds((db, RET_W), F32), sds((db, RET_W), F32)],
        compiler_params=pltpu.CompilerParams(vmem_limit_bytes=VMEM_LIMIT),
        name="inproj_sample",
    )(xs, tab_s, g0, b0, w1, qg, kg, wuq, wukt)

    qabs_t = jnp.transpose(qabs_s, (1, 0, 2))
    qr_s = q_s.reshape(db, MLA_HEADS, LANES)[:, :, NOPE_DIM:NOPE_DIM + ROPE_DIM]
    n_chunks = n_pages // PAGES_PER_CHUNK
    chunk_rows = PAGES_PER_CHUNK * PAGE_SIZE
    per_req = lambda s1, s2: pl.BlockSpec((1, s1, s2), lambda b, pt: (b, 0, 0))
    olat_s = pl.pallas_call(
        functools.partial(_attn_sample_kernel, n_chunks=n_chunks),
        grid_spec=pltpu.PrefetchScalarGridSpec(
            num_scalar_prefetch=1,
            grid=(db,),
            in_specs=[per_req(MLA_HEADS, KV_LORA), per_req(MLA_HEADS, ROPE_DIM), per_req(1, KV_LORA),
                      per_req(1, ROPE_DIM), _whole(pl.ANY), _whole(pl.ANY)],
            out_specs=per_req(MLA_HEADS, KV_LORA),
            scratch_shapes=[pltpu.VMEM((DMA_SLOTS, chunk_rows, KV_LORA), F32),
                            pltpu.VMEM((DMA_SLOTS, ROPE_DIM, chunk_rows), F32),
                            pltpu.VMEM((2, chunk_rows, KV_LORA), BF16),
                            pltpu.SemaphoreType.DMA((2, DMA_SLOTS))]),
        out_shape=sds((db, MLA_HEADS, KV_LORA), F32),
        compiler_params=_params(1),
        name="attn_sample",
    )(page_table, qabs_t, qr_s, c_s.reshape(db, 1, KV_LORA), kr_s.reshape(db, 1, ROPE_DIM),
      cache_kv_latent, jnp.swapaxes(cache_k_rope, 2, 3))

    rows = lambda w: pl.BlockSpec((RS_ROWS, w), lambda i: (i, 0))
    st_spec = pl.BlockSpec((RS_ROWS, RET_HEADS, RET_DK, RET_DV), lambda i: (i, 0, 0, 0))
    ro_s, st_s = pl.pallas_call(
        _ret_sample_kernel,
        grid=(db // RS_ROWS,),
        in_specs=[rows(RET_W), rows(RET_W), rows(RET_W), st_spec, wspec],
        out_specs=[rows(RET_W), st_spec],
        out_shape=[sds((db, RET_W), F32), sds((db, RET_HEADS, RET_DK, RET_DV), F32)],
        compiler_params=_params(1),
        name="ret_sample",
    )(rq_s, rk_s, rv_s, state_retention.reshape(db, RET_HEADS, RET_DK, RET_DV), gn)

    y_s, cs_s = pl.pallas_call(
        _merge_ffn_sample_kernel,
        in_specs=[wspec] * 19,
        out_specs=[wspec] * 2,
        out_shape=[sds((db, D_MODEL), F32), sds((db, (CONV_W - 1) * D_FF), F32)],
        compiler_params=pltpu.CompilerParams(vmem_limit_bytes=VMEM_LIMIT),
        name="merge_ffn_sample",
    )(xs, jnp.transpose(olat_s, (1, 0, 2)), ro_s, state_ffn_conv.reshape(db, (CONV_W - 1) * D_FF),
      g0, b0, w2, wuv3, wao, wro, wo, g1, b1, wfi, cw, cb, wfo, g2, b2)

    first = FRONT_PAD
    return (y_p,
            y_s.reshape(db, 1, D_MODEL),
            c_p[None, :, first:],
            kr_p[None, :, first:],
            st_p[None],
            cs_p[None],
            c_s.reshape(1, db, 1, KV_LORA),
            kr_s.reshape(1, db, 1, ROPE_DIM),
            st_s[None],
            cs_s.reshape(1, db, CONV_W - 1, D_FF))
```

```python
import functools

import numpy as np
import jax
import jax.numpy as jnp
from jax import lax
from jax.experimental import pallas as pl
from jax.experimental.pallas import tpu as pltpu

F32 = jnp.float32
BF16 = jnp.bfloat16

D_MODEL = 1024
N_META = 16
MLA_HEADS = 8
Q_LORA = 384
KV_LORA = 256
NOPE_DIM = 64
ROPE_DIM = 32
V_DIM = 64
ROPE_THETA = 10000.0
RET_HEADS = 4
RET_DK = 128
RET_DV = 128
D_FF = 2816
CONV_W = 3
PAGE_SIZE = 128
LN_EPS = 1e-5
RMS_EPS = 1e-6
DEPTH = 1
ALPHA = (2 * DEPTH) ** 0.25

LANES = 128
TILE = 128
FRONT_PAD = TILE - N_META
RET_W = RET_HEADS * RET_DK
HEAD_W = MLA_HEADS * LANES
N_PAIRS = MLA_HEADS // 2
ATTN_CLASSES = 6
N_TAB = 7
W1_COLS = Q_LORA + KV_LORA + 3 * RET_W + 2 * LANES
W2_COLS = RET_W + 2 * D_MODEL
NEG = -1e30
PAGES_PER_CHUNK = 32
IN_TM = 1024
IN_SUB = 512
MF_TM = 512
MF_SUB = 256
FF_CHUNKS = 2
RET_SEQS = 8
DMA_SLOTS = 4
VMEM_LIMIT = 56 * 1024 * 1024


def _dot(a, b):
    return jnp.dot(a, b, preferred_element_type=F32)


def _dot_nt(a, b):
    return lax.dot_general(a, b, (((1,), (1,)), ((), ())), preferred_element_type=F32)


def _layer_norm(x, g, b):
    mu = jnp.mean(x, -1, keepdims=True)
    xc = x - mu
    var = jnp.mean(xc * xc, -1, keepdims=True)
    return xc * lax.rsqrt(var + LN_EPS) * g + b


def _rms_norm(x, g):
    return x * lax.rsqrt(jnp.mean(x * x, -1, keepdims=True) + RMS_EPS) * g


def _group_norm(o):
    mu = jnp.mean(o, -1, keepdims=True)
    oc = o - mu
    var = jnp.mean(oc * oc, -1, keepdims=True)
    return oc * lax.rsqrt(var + LN_EPS)


def _log_gamma(h):
    return float(np.log(np.float32(1.0) - np.float32(2.0) ** np.float32(-5.0 - h), dtype=np.float32))


def _exp32(v):
    return float(np.exp(np.float32(v), dtype=np.float32))


def _whole(memory_space=pltpu.VMEM):
    return pl.BlockSpec(memory_space=memory_space)


def _params(n_axes):
    return pltpu.CompilerParams(dimension_semantics=("arbitrary",) * n_axes, vmem_limit_bytes=VMEM_LIMIT)


def _inproj_common(x, tab, g0, b0, w1, qg, kg, wuq):
    xn = _layer_norm(x, g0, b0)
    h = _dot(xn.astype(BF16), w1)
    o = 0
    q_lat = h[:, o:o + Q_LORA]; o += Q_LORA
    c_raw = h[:, o:o + KV_LORA]; o += KV_LORA
    rq = h[:, o:o + RET_W]; o += RET_W
    rk = h[:, o:o + RET_W]; o += RET_W
    rv = h[:, o:o + RET_W]; o += RET_W
    kr = h[:, o:o + LANES]; o += LANES
    krs = h[:, o:o + LANES]
    qc, qs1, qs2, rc, rs, kc, ks = [tab[:, i * LANES:(i + 1) * LANES] for i in range(N_TAB)]
    c = _rms_norm(c_raw, kg)
    krope = kr * kc + krs * ks
    q = _dot(_rms_norm(q_lat, qg).astype(BF16), wuq)
    q_heads = []
    for hh in range(MLA_HEADS):
        qh = q[:, hh * LANES:(hh + 1) * LANES]
        q_heads.append(qh * qc + pltpu.roll(qh, ROPE_DIM // 2, 1) * qs1
                       + pltpu.roll(qh, LANES - ROPE_DIM // 2, 1) * qs2)

    def rope_ret(x):
        return [x[:, hh * RET_DK:(hh + 1) * RET_DK] * rc
                + pltpu.roll(x[:, hh * RET_DK:(hh + 1) * RET_DK], RET_DK // 2, 1) * rs
                for hh in range(RET_HEADS)]

    rq_h = rope_ret(rq)
    rk_h = [v * (RET_DK ** -0.5) for v in rope_ret(rk)]
    return q_heads, c, krope, rq_h, rk_h, rv


def _prompt_rows(x_refs, meta_ref, tab_ref, first_piece, n_pieces, nt):
    row = lax.broadcasted_iota(jnp.int32, (TILE, 1), 0)
    xs, tabs, pos = [], [], []
    for j in range(first_piece, first_piece + n_pieces):
        t = (pl.program_id(0) * len(x_refs) + j) % nt
        xs.append(jnp.where(t == 0, meta_ref[...], x_refs[j][0]))
        p0 = pl.multiple_of(t * TILE, TILE)
        if tab_ref is not None:
            tabs.append(tab_ref[pl.ds(p0, TILE), :])
        pos.append(p0 + row)
    cat = lambda parts: parts[0] if len(parts) == 1 else jnp.concatenate(parts, 0)
    return cat(xs), (cat(tabs) if tabs else None), cat(pos)


def _inproj_prompt_kernel(*refs, nt, n_x):
    x_refs = refs[:n_x]
    (meta_ref, tab_ref, g0, b0, w1, qg, kg, wuq, wukp, e8, wuv,
     q_o, k_o, v_o, c_o, kr_o, rq_o, rk_o, rv_o) = refs[n_x:]
    per_group = min(IN_SUB // TILE, n_x)
    for s in range(n_x // per_group):
        rows = slice(s * per_group * TILE, (s + 1) * per_group * TILE)
        x, tab, pos = _prompt_rows(x_refs, meta_ref, tab_ref, s * per_group, per_group, nt)
        real = pos >= FRONT_PAD
        q_heads, c, krope, rq_h, rk_h, rv = _inproj_common(
            x, tab, g0[...], b0[...], w1[...], qg[...], kg[...], wuq[...])
        for hh in range(MLA_HEADS):
            q_o[hh // 2, rows, (hh % 2) * LANES:(hh % 2 + 1) * LANES] = q_heads[hh].astype(BF16)
        for hh in range(RET_HEADS):
            sl = slice(hh * RET_DK, (hh + 1) * RET_DK)
            rq_o[rows, sl] = rq_h[hh]
            rk_o[rows, sl] = jnp.where(real, rk_h[hh], 0.0)
        rv_o[rows, :] = rv
        cb = c.astype(BF16)
        k = (_dot(cb, wukp[...]) + _dot(krope.astype(BF16), e8[...])).astype(BF16)
        v = _dot(cb, wuv[...]).astype(BF16)
        for pr in range(N_PAIRS):
            k_o[pr, rows, :] = k[:, pr * 2 * LANES:(pr + 1) * 2 * LANES]
            v_o[pr, rows, :] = v[:, pr * 2 * V_DIM:(pr + 1) * 2 * V_DIM]
        c_o[rows, :] = c
        kr_o[rows, :] = krope[:, :ROPE_DIM]


def _inproj_sample_kernel(x_ref, tab_ref, g0, b0, w1, qg, kg, wuq, wukt,
                          q_o, qabs_o, c_o, kr_o, rq_o, rk_o, rv_o):
    q_heads, c, krope, rq_h, rk_h, rv = _inproj_common(
        x_ref[...], tab_ref[...], g0[...], b0[...], w1[...], qg[...], kg[...], wuq[...])
    for hh in range(MLA_HEADS):
        qb = q_heads[hh].astype(BF16)
        q_o[:, hh * LANES:(hh + 1) * LANES] = qb
        qabs_o[hh] = _dot(qb, wukt[hh]).astype(BF16)
    for hh in range(RET_HEADS):
        sl = slice(hh * RET_DK, (hh + 1) * RET_DK)
        rq_o[:, sl] = rq_h[hh]
        rk_o[:, sl] = rk_h[hh]
    rv_o[...] = rv
    c_o[...] = c
    kr_o[...] = krope[:, :ROPE_DIM]


def _attn_prompt_kernel(q_ref, k_ref, v_ref, o_ref, *, bounds):
    qi = pl.program_id(1)
    row = lax.broadcasted_iota(jnp.int32, (TILE, TILE), 0) + qi * TILE
    lane = lax.broadcasted_iota(jnp.int32, (TILE, TILE), 1)
    first = lane < V_DIM

    def run(lo, wt):
        width = wt * TILE

        def pair_body(pr, carry):
            q = q_ref[pr]
            k = k_ref[pr, :width, :]
            v = v_ref[pr, :width, :]
            probs, inv_l = [], []
            for hh in range(2):
                s = _dot_nt(q[:, hh * LANES:(hh + 1) * LANES], k[:, hh * LANES:(hh + 1) * LANES])
                tiles = []
                for c in range(wt):
                    sc = s[:, c * TILE:(c + 1) * TILE]
                    if c == 0:
                        sc = jnp.where(lane >= FRONT_PAD, sc, NEG)
                    if c >= lo:
                        sc = jnp.where(lane + c * TILE <= row, sc, NEG)
                    tiles.append(sc)
                s = jnp.concatenate(tiles, 1)
                p = jnp.exp(s - jnp.max(s, -1, keepdims=True))
                inv_l.append(1.0 / jnp.sum(p, -1, keepdims=True))
                probs.append(p.astype(BF16))
            pv = _dot(jnp.concatenate(probs, 0), v)
            o_ref[pr] = jnp.where(first, pv[:TILE] * inv_l[0], pv[TILE:] * inv_l[1]).astype(BF16)
            return carry

        lax.fori_loop(0, N_PAIRS, pair_body, 0, unroll=True)

    lo = 0
    for wt in bounds:
        pl.when((qi >= lo) & (qi < wt))(functools.partial(run, lo, wt))
        lo = wt


def _ret_prompt_kernel(rq_ref, rk_ref, rv_ref, gn_ref, o_ref, st_ref, state):
    t = pl.program_id(1)

    @pl.when(t == 0)
    def _():
        state[...] = jnp.zeros_like(state)

    ii = lax.broadcasted_iota(jnp.int32, (TILE, TILE), 0).astype(F32)
    jj = lax.broadcasted_iota(jnp.int32, (TILE, TILE), 1).astype(F32)
    diff = ii - jj
    idx = lax.broadcasted_iota(jnp.int32, (TILE, 1), 0).astype(F32)
    pw = 2 * RET_DK
    left = lax.broadcasted_iota(jnp.int32, (TILE, pw), 1) < RET_DK
    top = lax.broadcasted_iota(jnp.int32, (pw, 1), 0) < RET_DK
    same_head = top == (lax.broadcasted_iota(jnp.int32, (pw, pw), 1) < RET_DV)

    def block_diag(x):
        return jnp.concatenate([jnp.where(left, x, 0.0), jnp.where(left, 0.0, x)], 0).astype(BF16)

    for pp in range(RET_HEADS // 2):
        sl = slice(pp * pw, (pp + 1) * pw)
        lg = (_log_gamma(2 * pp), _log_gamma(2 * pp + 1))
        decay = jnp.concatenate([jnp.where(diff >= 0, jnp.exp(jnp.maximum(diff, 0.0) * g), 0.0) for g in lg], 1)
        q_dec = jnp.where(left, jnp.exp((idx + 1.0) * lg[0]), jnp.exp((idx + 1.0) * lg[1]))
        k_decay = jnp.where(left, jnp.exp((TILE - 1.0 - idx) * lg[0]), jnp.exp((TILE - 1.0 - idx) * lg[1]))
        chunk_decay = jnp.where(top, _exp32(TILE * lg[0]), _exp32(TILE * lg[1]))
        for bb in range(rq_ref.shape[0]):
            q = rq_ref[bb, :, sl]
            k = rk_ref[bb, :, sl]
            v = rv_ref[bb, :, sl]
            st = state[bb, pp]
            inner = _dot_nt(q.astype(BF16), block_diag(k)) * decay
            o = _dot(inner.astype(BF16), block_diag(v)) + _dot((q * q_dec).astype(BF16), st.astype(BF16))
            kv = _dot((k * k_decay).T.astype(BF16), v.astype(BF16))
            state[bb, pp] = chunk_decay * st + jnp.where(same_head, kv, 0.0)
            for h2 in range(2):
                hs = slice(pp * pw + h2 * RET_DV, pp * pw + (h2 + 1) * RET_DV)
                o_ref[bb, :, hs] = _group_norm(o[:, h2 * RET_DV:(h2 + 1) * RET_DV]) * gn_ref[:, hs]

    @pl.when(t == pl.num_programs(1) - 1)
    def _():
        for pp in range(RET_HEADS // 2):
            for h2 in range(2):
                st_ref[:, 2 * pp + h2] = state[:, pp, h2 * RET_DK:(h2 + 1) * RET_DK, h2 * RET_DV:(h2 + 1) * RET_DV]


def _merge(xn, attn_proj, ron, w2, wro, wo, g1, b1):
    g3 = _dot(xn.astype(BF16), w2)
    rg = g3[:, :RET_W]
    ga = g3[:, RET_W:RET_W + D_MODEL]
    gb = g3[:, RET_W + D_MODEL:]
    ret = (ron * (rg * jax.nn.sigmoid(rg))).astype(BF16)
    m = jax.nn.sigmoid(ga) * attn_proj + jax.nn.sigmoid(gb) * _dot(ret, wro)
    return _layer_norm(ALPHA * xn + _dot(m.astype(BF16), wo), g1, b1)


def _ffn_tail(x1, a, a1, a2, g, cw, cb, wfo, g2, b2):
    acc = cb + cw[0:1] * a2 + cw[1:2] * a1 + cw[2:3] * a
    hid = (jax.nn.gelu(acc) * g).astype(BF16)
    return _layer_norm(ALPHA * x1 + _dot(hid, wfo), g2, b2)


def _merge_ffn_prompt_kernel(*refs, nb, nt, n_x):
    x_refs = refs[:n_x]
    (meta_ref, ao_ref, ro_ref, g0, b0, w2, wao, wro, wo, g1, b1,
     wfi, cw, cb, wfo, g2, b2, y_hbm, cs_ref, carry, ybuf, ysem) = refs[n_x:]
    i = pl.program_id(0)
    last = pl.num_programs(0) - 1
    tm = n_x * TILE
    plen = nt * TILE

    def y_copies(step):
        out = []
        for j in range(n_x):
            g = step * n_x + j
            t = g % nt
            rows = pl.ds(pl.multiple_of(jnp.maximum(t - 1, 0) * TILE, TILE), TILE)
            out.append((t > 0, pltpu.make_async_copy(ybuf.at[step % 2, pl.ds(j * TILE, TILE)],
                                                     y_hbm.at[g // nt, rows], ysem.at[step % 2])))
        return out

    def y_wait(step):
        for is_prompt, cp in y_copies(step):
            pl.when(is_prompt)(cp.wait)

    @pl.when(i == 0)
    def _():
        carry[...] = jnp.zeros_like(carry)

    pl.when(i >= 2)(functools.partial(y_wait, i - 2))

    prev = carry[...]
    a_parts = []
    sub = min(MF_SUB, tm)
    for s in range(tm // sub):
        rows = slice(s * sub, (s + 1) * sub)
        x, _, pos = _prompt_rows(x_refs, meta_ref, None, s * (sub // TILE), sub // TILE, nt)
        xn = _layer_norm(x, g0[...], b0[...])
        attn_proj = None
        for pr in range(N_PAIRS):
            part = _dot(ao_ref[pr, rows, :], wao[pr * 2 * V_DIM:(pr + 1) * 2 * V_DIM, :])
            attn_proj = part if attn_proj is None else attn_proj + part
        x1 = _merge(xn, attn_proj, ro_ref[rows, :], w2[...], wro[...], wo[...], g1[...], b1[...])
        x1b = x1.astype(BF16)
        row = lax.broadcasted_iota(jnp.int32, (sub, 1), 0)
        ffn, lasts = None, []
        for c in range(FF_CHUNKS):
            lo_c = c * (D_FF // FF_CHUNKS)
            cs = slice(lo_c, lo_c + D_FF // FF_CHUNKS)
            a = _dot(x1b, wfi[:, cs])
            g = _dot(x1b, wfi[:, D_FF + lo_c:D_FF + lo_c + D_FF // FF_CHUNKS])
            pv = prev[:, cs]
            a1 = jnp.where(row == 0, pv[7:8], pltpu.roll(a, 1, 0))
            a2 = jnp.where(row == 0, pv[6:7], jnp.where(row == 1, pv[7:8], pltpu.roll(a, 2, 0)))
            a1 = jnp.where(pos >= FRONT_PAD + 1, a1, 0.0)
            a2 = jnp.where(pos >= FRONT_PAD + 2, a2, 0.0)
            acc = cb[:, cs] + cw[0:1, cs] * a2 + cw[1:2, cs] * a1 + cw[2:3, cs] * a
            part = _dot((jax.nn.gelu(acc) * g).astype(BF16), wfo[cs, :])
            ffn = part if ffn is None else ffn + part
            lasts.append([a[k * TILE - 8:k * TILE] for k in range(1, sub // TILE + 1)])
        ybuf[i % 2, rows, :] = _layer_norm(ALPHA * x1 + ffn, g2[...], b2[...])
        for k in range(sub // TILE):
            a_parts.append(jnp.concatenate([lasts[c][k] for c in range(FF_CHUNKS)], 1))
        prev = a_parts[-1]
    carry[...] = prev
    for is_prompt, cp in y_copies(i):
        pl.when(is_prompt)(cp.start)

    @pl.when(i == last)
    def _():
        pl.when(i >= 1)(functools.partial(y_wait, i - 1))
        y_wait(i)

    for b in range(nb):
        end = (b + 1) * plen
        local = (end - 1) % tm + 1

        @pl.when(i == (end - 1) // tm)
        def _():
            cs_ref[b] = a_parts[local // TILE - 1][8 - (CONV_W - 1):]


def _merge_ffn_sample_kernel(x_ref, olat_ref, ro_ref, prev_ref, g0, b0, w2, wuv3, wao, wro, wo, g1, b1,
                             wfi, cw, cb, wfo, g2, b2, y_ref, cs_ref):
    xn = _layer_norm(x_ref[...], g0[...], b0[...])
    attn_proj = None
    for hh in range(MLA_HEADS):
        oh = _dot(olat_ref[hh].astype(BF16), wuv3[hh]).astype(BF16)
        part = _dot(oh, wao[hh * V_DIM:(hh + 1) * V_DIM, :])
        attn_proj = part if attn_proj is None else attn_proj + part
    x1 = _merge(xn, attn_proj, ro_ref[...], w2[...], wro[...], wo[...], g1[...], b1[...])
    x1b = x1.astype(BF16)
    a = _dot(x1b, wfi[:, :D_FF])
    g = _dot(x1b, wfi[:, D_FF:])
    prev0 = prev_ref[:, :D_FF]
    prev1 = prev_ref[:, D_FF:]
    y_ref[...] = _ffn_tail(x1, a, prev1, prev0, g, cw[...], cb[...], wfo[...], g2[...], b2[...])
    cs_ref[:, :D_FF] = prev1
    cs_ref[:, D_FF:] = a


def _attn_sample_kernel(pt_ref, qabs_ref, qr_ref, cnew_ref, krnew_ref, cache_c, cache_kr, o_ref,
                        cbuf, krbuf, cb16, sems, *, n_chunks):
    b = pl.program_id(0)
    nb = pl.num_programs(0)

    def copies(bb, ci, slot):
        out = []
        for p in range(PAGES_PER_CHUNK):
            page = pt_ref[bb, ci * PAGES_PER_CHUNK + p]
            rows = pl.ds(p * PAGE_SIZE, PAGE_SIZE)
            out.append((pltpu.make_async_copy(cache_c.at[0, page], cbuf.at[slot, rows], sems.at[0, slot]), p % 2))
            out.append((pltpu.make_async_copy(cache_kr.at[0, page], krbuf.at[slot, :, rows], sems.at[1, slot]),
                        (p + 1) % 2))
        return out

    def start(bb, ci, slot):
        for cp, prio in copies(bb, ci, slot):
            cp.start(priority=prio)

    ahead = DMA_SLOTS - 1
    ring = lambda g: lax.rem(g, DMA_SLOTS)

    @pl.when(b == 0)
    def _():
        for g0 in range(ahead):
            start(g0 // n_chunks, g0 % n_chunks, g0)

    qabs = qabs_ref[0]
    qr = qr_ref[0]
    m = jnp.full((MLA_HEADS, 1), NEG, F32)
    l = jnp.zeros((MLA_HEADS, 1), F32)
    acc = jnp.zeros((MLA_HEADS, KV_LORA), F32)
    pending = None
    for ci in range(n_chunks):
        g = b * n_chunks + ci
        slot = ring(g)
        half = ci % 2
        later, ci_ahead = divmod(ci + ahead, n_chunks)
        if later == 0:
            start(b, ci_ahead, ring(g + ahead))
        else:
            pl.when(b + later < nb)(functools.partial(start, b + later, ci_ahead, ring(g + ahead)))
        for cp, _ in copies(b, ci, slot):
            cp.wait()
        cb16[half] = cbuf[slot].astype(BF16)
        s = _dot_nt(qabs, cb16[half]) + _dot(qr, krbuf[slot].astype(BF16))
        if pending is not None:
            acc = pending[1] * acc + _dot(pending[0], cb16[pending[2]])
        m_new = jnp.maximum(m, jnp.max(s, -1, keepdims=True))
        p = jnp.exp(s - m_new)
        alpha = jnp.exp(m - m_new)
        l = alpha * l + jnp.sum(p, -1, keepdims=True)
        m = m_new
        pending = (p.astype(BF16), alpha, half)
    acc = pending[1] * acc + _dot(pending[0], cb16[pending[2]])
    cnew = cnew_ref[0]
    s_new = (jnp.sum(qabs.astype(F32) * cnew, -1, keepdims=True)
             + jnp.sum(qr.astype(F32) * krnew_ref[0], -1, keepdims=True))
    m_new = jnp.maximum(m, s_new)
    p_new = jnp.exp(s_new - m_new)
    alpha = jnp.exp(m - m_new)
    o_ref[0] = (alpha * acc + p_new * cnew) / (alpha * l + p_new)


RS_ROWS = 8


def _ret_sample_kernel(rq_ref, rk_ref, rv_ref, st_ref, gn_ref, o_ref, ns_ref):
    for hh in range(RET_HEADS):
        sl = slice(hh * RET_DK, (hh + 1) * RET_DK)
        gamma = _exp32(_log_gamma(hh))
        q8 = rq_ref[:, sl]
        k8 = rk_ref[:, sl]
        v8 = rv_ref[:, sl]
        qk = jnp.sum(q8 * k8, -1, keepdims=True)
        for r in range(RS_ROWS):
            kcol = jnp.broadcast_to(k8[r:r + 1], (RET_DK, RET_DK)).T
            qcol = jnp.broadcast_to(q8[r:r + 1], (RET_DK, RET_DK)).T
            st = st_ref[r, hh]
            v = v8[r:r + 1]
            ns_ref[r, hh] = gamma * st + kcol * v
            o = qk[r:r + 1] * v + jnp.sum(qcol * gamma * st, 0, keepdims=True)
            o_ref[r:r + 1, sl] = _group_norm(o) * gn_ref[:, sl]


def _rope_tables(pos):
    pos = np.asarray(pos, np.float64)
    n = pos.shape[0]

    def cos_sin(half):
        inv = np.float64(ROPE_THETA) ** (-np.arange(half, dtype=np.float64) / half)
        ang = pos[:, None] * inv[None, :]
        return np.cos(ang), np.sin(ang)

    c16, s16 = cos_sin(ROPE_DIM // 2)
    c64, s64 = cos_sin(RET_DK // 2)
    z = lambda w: np.zeros((n, w), np.float64)
    scale = (NOPE_DIM + ROPE_DIM) ** -0.5
    tail = LANES - NOPE_DIM - ROPE_DIM
    qc = np.concatenate([np.ones((n, NOPE_DIM), np.float64), c16, c16, z(tail)], 1) * scale
    qs1 = np.concatenate([z(NOPE_DIM + ROPE_DIM // 2), s16, z(tail)], 1) * scale
    qs2 = np.concatenate([z(NOPE_DIM), -s16, z(ROPE_DIM // 2 + tail)], 1) * scale
    rc = np.concatenate([c64, c64], 1)
    rs = np.concatenate([-s64, s64], 1)
    kc = np.concatenate([c16, c16, z(LANES - ROPE_DIM)], 1)
    ks = np.concatenate([-s16, s16, z(LANES - ROPE_DIM)], 1)
    return np.concatenate([qc, qs1, qs2, rc, rs, kc, ks], 1).astype(np.float32)


def _prep_weights(w_in, w_uq, w_uk, w_uv, w_ffn_in):
    sizes = (Q_LORA, KV_LORA, ROPE_DIM, RET_W, RET_W, RET_W, RET_W, D_MODEL, D_MODEL)
    offs = [0]
    for s in sizes:
        offs.append(offs[-1] + s)
    col = lambda i: w_in[:, offs[i]:offs[i + 1]]
    w_kr = col(2)
    half = ROPE_DIM // 2
    zpad = jnp.zeros((D_MODEL, LANES - ROPE_DIM), F32)
    w1 = jnp.concatenate([col(0), col(1), col(3), col(4), col(5), w_kr, zpad,
                          w_kr[:, half:], w_kr[:, :half], zpad], 1).astype(BF16)
    w2 = jnp.concatenate([col(6), col(7), col(8)], 1).astype(BF16)
    hd = NOPE_DIM + ROPE_DIM
    wuq = jnp.pad(w_uq, ((0, 0), (0, 0), (0, LANES - hd))).reshape(Q_LORA, HEAD_W).astype(BF16)
    wukp = jnp.pad(w_uk, ((0, 0), (0, 0), (0, LANES - NOPE_DIM))).reshape(KV_LORA, HEAD_W).astype(BF16)
    e1 = jnp.pad(jnp.eye(ROPE_DIM, dtype=F32), ((0, LANES - ROPE_DIM), (NOPE_DIM, LANES - hd)))
    e8 = jnp.tile(e1, (1, MLA_HEADS)).astype(BF16)
    wuv = w_uv.reshape(KV_LORA, MLA_HEADS * V_DIM).astype(BF16)
    wukt = jnp.pad(jnp.transpose(w_uk, (1, 2, 0)), ((0, 0), (0, LANES - NOPE_DIM), (0, 0))).astype(BF16)
    wuv3 = jnp.transpose(w_uv, (1, 0, 2)).astype(BF16)
    return w1, w2, wuq, wukp, e8, wuv, wukt, wuv3, w_ffn_in.astype(BF16)


def kernel(x_prompt, x_sample, cache_kv_latent, cache_k_rope, state_retention, state_ffn_conv, page_table, meta_tokens, ln0_g, ln0_b, w_in, q_norm_g, kv_norm_g, w_uq, w_uk, w_uv, ret_gn_g, w_attn_out, w_ret_out, w_o, ln1_g, ln1_b, w_ffn_in, conv_w, conv_b, w_ffn_out, ln2_g, ln2_b):
    assert w_in.shape[0] == DEPTH == 1
    nb, seq, _ = x_prompt.shape
    db, dec_seq, _ = x_sample.shape
    n_pages = page_table.shape[1]
    assert seq % TILE == 0 and dec_seq == 1 and db % RS_ROWS == 0
    assert n_pages % (2 * PAGES_PER_CHUNK) == 0 and db * (n_pages // PAGES_PER_CHUNK) >= DMA_SLOTS - 1
    nt = seq // TILE + 1
    plen = nt * TILE
    past = n_pages * PAGE_SIZE

    w1, w2, wuq, wukp, e8, wuv, wukt, wuv3, wfi = _prep_weights(w_in[0], w_uq[0], w_uk[0], w_uv[0], w_ffn_in[0])
    wao = w_attn_out[0].astype(BF16)
    wro = w_ret_out[0].astype(BF16)
    wo = w_o[0].astype(BF16)
    wfo = w_ffn_out[0].astype(BF16)
    row = lambda v: v.reshape(1, -1)
    g0, b0 = row(ln0_g), row(ln0_b)
    g1, b1, g2, b2 = row(ln1_g[0]), row(ln1_b[0]), row(ln2_g[0]), row(ln2_b[0])
    qg, kg, gn = row(q_norm_g[0]), row(kv_norm_g[0]), row(ret_gn_g[0])
    cw, cb = conv_w[0], row(conv_b[0])

    tab_p = jnp.asarray(_rope_tables(np.maximum(np.arange(plen) - FRONT_PAD, 0)))
    tab_s = jnp.broadcast_to(jnp.asarray(_rope_tables([past])), (db, N_TAB * LANES))
    n_rows = nb * plen
    meta_tile = jnp.concatenate([jnp.zeros((FRONT_PAD, D_MODEL), F32), meta_tokens.astype(F32)], 0)
    in_tm = IN_TM if n_rows % IN_TM == 0 else TILE
    mf_tm = MF_TM if n_rows % MF_TM == 0 else TILE

    def x_specs(tm):
        def spec(j):
            def index(i):
                g = i * (tm // TILE) + j
                return g // nt, jnp.maximum(g % nt - 1, 0), 0
            return pl.BlockSpec((1, TILE, D_MODEL), index)
        return [spec(j) for j in range(tm // TILE)]

    flat = lambda tm, w: pl.BlockSpec((tm, w), lambda i: (i, 0))
    pair_flat = lambda tm, w: pl.BlockSpec((N_PAIRS, tm, w), lambda i: (0, i, 0))
    pair_tok = lambda w: pl.BlockSpec((N_PAIRS, TILE, w), lambda b, t: (0, b * nt + t, 0))
    pair_seq = lambda w: pl.BlockSpec((N_PAIRS, plen, w), lambda b, t: (0, b, 0))
    wspec = _whole()
    sds = jax.ShapeDtypeStruct

    q_p, k_p, v_p, c_p, kr_p, rq_p, rk_p, rv_p = pl.pallas_call(
        functools.partial(_inproj_prompt_kernel, nt=nt, n_x=in_tm // TILE),
        grid=(n_rows // in_tm,),
        in_specs=x_specs(in_tm) + [wspec] * 11,
        out_specs=[pair_flat(in_tm, 2 * LANES), pair_flat(in_tm, 2 * LANES), pair_flat(in_tm, 2 * V_DIM),
                   flat(in_tm, KV_LORA), flat(in_tm, ROPE_DIM), flat(in_tm, RET_W), flat(in_tm, RET_W),
                   flat(in_tm, RET_W)],
        out_shape=[sds((N_PAIRS, n_rows, 2 * LANES), BF16), sds((N_PAIRS, n_rows, 2 * LANES), BF16),
                   sds((N_PAIRS, n_rows, 2 * V_DIM), BF16), sds((n_rows, KV_LORA), F32),
                   sds((n_rows, ROPE_DIM), F32), sds((n_rows, RET_W), F32),
                   sds((n_rows, RET_W), F32), sds((n_rows, RET_W), F32)],
        compiler_params=_params(1),
        name="inproj_prompt",
    )(*([x_prompt] * (in_tm // TILE)), meta_tile, tab_p, g0, b0, w1, qg, kg, wuq, wukp, e8, wuv)

    bounds = tuple(sorted({-(-nt * c // ATTN_CLASSES) for c in range(1, ATTN_CLASSES + 1)}))
    ao_p = pl.pallas_call(
        functools.partial(_attn_prompt_kernel, bounds=bounds),
        grid=(nb, nt),
        in_specs=[pair_tok(2 * LANES), pair_seq(2 * LANES), pair_seq(2 * V_DIM)],
        out_specs=pair_tok(2 * V_DIM),
        out_shape=sds((N_PAIRS, n_rows, 2 * V_DIM), BF16),
        compiler_params=_params(2),
        name="attn_prompt",
    )(q_p, k_p, v_p)

    seq3 = lambda v: v.reshape(nb, plen, v.shape[-1])
    rb = RET_SEQS if nb % RET_SEQS == 0 else 1
    tok = lambda w: pl.BlockSpec((rb, TILE, w), lambda b, t: (b, t, 0))
    ro_p, st_p = pl.pallas_call(
        _ret_prompt_kernel,
        grid=(nb // rb, nt),
        in_specs=[tok(RET_W), tok(RET_W), tok(RET_W), wspec],
        out_specs=[tok(RET_W), pl.BlockSpec((rb, RET_HEADS, RET_DK, RET_DV), lambda b, t: (b, 0, 0, 0))],
        out_shape=[sds((nb, plen, RET_W), F32), sds((nb, RET_HEADS, RET_DK, RET_DV), F32)],
        scratch_shapes=[pltpu.VMEM((rb, RET_HEADS // 2, 2 * RET_DK, 2 * RET_DV), F32)],
        compiler_params=_params(2),
        name="ret_prompt",
    )(seq3(rq_p), seq3(rk_p), seq3(rv_p), gn)

    y_p, cs_p = pl.pallas_call(
        functools.partial(_merge_ffn_prompt_kernel, nb=nb, nt=nt, n_x=mf_tm // TILE),
        grid=(n_rows // mf_tm,),
        in_specs=x_specs(mf_tm) + [wspec, pair_flat(mf_tm, 2 * V_DIM), flat(mf_tm, RET_W)] + [wspec] * 14,
        out_specs=[_whole(pl.ANY), wspec],
        out_shape=[sds((nb, seq, D_MODEL), F32), sds((nb, CONV_W - 1, D_FF), F32)],
        scratch_shapes=[pltpu.VMEM((8, D_FF), F32), pltpu.VMEM((2, mf_tm, D_MODEL), F32),
                        pltpu.SemaphoreType.DMA((2,))],
        compiler_params=_params(1),
        name="merge_ffn_prompt",
    )(*([x_prompt] * (mf_tm // TILE)), meta_tile, ao_p, ro_p.reshape(n_rows, RET_W),
      g0, b0, w2, wao, wro, wo, g1, b1, wfi, cw, cb, wfo, g2, b2)
    c_p = seq3(c_p)
    kr_p = seq3(kr_p)

    xs = x_sample.reshape(db, D_MODEL)
    q_s, qabs_s, c_s, kr_s, rq_s, rk_s, rv_s = pl.pallas_call(
        _inproj_sample_kernel,
        in_specs=[wspec] * 9,
        out_specs=[wspec] * 7,
        out_shape=[sds((db, HEAD_W), BF16), sds((MLA_HEADS, db, KV_LORA), BF16), sds((db, KV_LORA), F32),
                   sds((db, ROPE_DIM), F32), sds((db, RET_W), F32), sds((db, RET_W), F32), sds((db, RET_W), F32)],
        compiler_params=pltpu.CompilerParams(vmem_limit_bytes=VMEM_LIMIT),
        name="inproj_sample",
    )(xs, tab_s, g0, b0, w1, qg, kg, wuq, wukt)

    qabs_t = jnp.transpose(qabs_s, (1, 0, 2))
    qr_s = q_s.reshape(db, MLA_HEADS, LANES)[:, :, NOPE_DIM:NOPE_DIM + ROPE_DIM]
    n_chunks = n_pages // PAGES_PER_CHUNK
    chunk_rows = PAGES_PER_CHUNK * PAGE_SIZE
    per_req = lambda s1, s2: pl.BlockSpec((1, s1, s2), lambda b, pt: (b, 0, 0))
    olat_s = pl.pallas_call(
        functools.partial(_attn_sample_kernel, n_chunks=n_chunks),
        grid_spec=pltpu.PrefetchScalarGridSpec(
            num_scalar_prefetch=1,
            grid=(db,),
            in_specs=[per_req(MLA_HEADS, KV_LORA), per_req(MLA_HEADS, ROPE_DIM), per_req(1, KV_LORA),
                      per_req(1, ROPE_DIM), _whole(pl.ANY), _whole(pl.ANY)],
            out_specs=per_req(MLA_HEADS, KV_LORA),
            scratch_shapes=[pltpu.VMEM((DMA_SLOTS, chunk_rows, KV_LORA), F32),
                            pltpu.VMEM((DMA_SLOTS, ROPE_DIM, chunk_rows), F32),
                            pltpu.VMEM((2, chunk_rows, KV_LORA), BF16),
                            pltpu.SemaphoreType.DMA((2, DMA_SLOTS))]),
        out_shape=sds((db, MLA_HEADS, KV_LORA), F32),
        compiler_params=_params(1),
        name="attn_sample",
    )(page_table, qabs_t, qr_s, c_s.reshape(db, 1, KV_LORA), kr_s.reshape(db, 1, ROPE_DIM),
      cache_kv_latent, jnp.swapaxes(cache_k_rope, 2, 3))

    rows = lambda w: pl.BlockSpec((RS_ROWS, w), lambda i: (i, 0))
    st_spec = pl.BlockSpec((RS_ROWS, RET_HEADS, RET_DK, RET_DV), lambda i: (i, 0, 0, 0))
    ro_s, st_s = pl.pallas_call(
        _ret_sample_kernel,
        grid=(db // RS_ROWS,),
        in_specs=[rows(RET_W), rows(RET_W), rows(RET_W), st_spec, wspec],
        out_specs=[rows(RET_W), st_spec],
        out_shape=[sds((db, RET_W), F32), sds((db, RET_HEADS, RET_DK, RET_DV), F32)],
        compiler_params=_params(1),
        name="ret_sample",
    )(rq_s, rk_s, rv_s, state_retention.reshape(db, RET_HEADS, RET_DK, RET_DV), gn)

    y_s, cs_s = pl.pallas_call(
        _merge_ffn_sample_kernel,
        in_specs=[wspec] * 19,
        out_specs=[wspec] * 2,
        out_shape=[sds((db, D_MODEL), F32), sds((db, (CONV_W - 1) * D_FF), F32)],
        compiler_params=pltpu.CompilerParams(vmem_limit_bytes=VMEM_LIMIT),
        name="merge_ffn_sample",
    )(xs, jnp.transpose(olat_s, (1, 0, 2)), ro_s, state_ffn_conv.reshape(db, (CONV_W - 1) * D_FF),
      g0, b0, w2, wuv3, wao, wro, wo, g1, b1, wfi, cw, cb, wfo, g2, b2)

    first = FRONT_PAD
    return (y_p,
            y_s.reshape(db, 1, D_MODEL),
            c_p[None, :, first:],
            kr_p[None, :, first:],
            st_p[None],
            cs_p[None],
            c_s.reshape(1, db, 1, KV_LORA),
            kr_s.reshape(1, db, 1, ROPE_DIM),
            st_s[None],
            cs_s.reshape(1, db, CONV_W - 1, D_FF))
```
